```python
import jax, jax.numpy as jnp
from jax import lax
import numpy as np

D_MODEL = 1024
BATCH = 8
SEQ = 2048
DEPTH = 1
DEC_BATCH = 128
DEC_SEQ = 4
PAST_LEN = 16384
PAGE_SIZE = 128

N_META = 16
D_MIX = 2 * D_MODEL
D_CONV_GRP = D_MODEL
D_SSM = D_MIX - D_CONV_GRP
SSM_HEAD_DIM = 64
SSM_HEADS = D_SSM // SSM_HEAD_DIM
SSM_GROUPS = 2
SSM_STATE = 128
SSM_CONV = 4
CHUNK = 128
CONF_KERNEL = 31
FFN_KERNEL = 3
D_FF = 2816
GN = SSM_GROUPS * SSM_STATE
XBC_DIM = D_SSM + 2 * GN
OFF_Z = 2 * D_CONV_GRP
OFF_XBC = OFF_Z + D_SSM
OFF_DT = OFF_XBC + XBC_DIM
IN_PROJ_DIM = OFF_DT + SSM_HEADS
EPS = 1e-5

kernel_name = "hymba_conformer_ssd_convffn_step"


def _rmsnorm(x, g):
    xf = x.astype(jnp.float32)
    y = xf * lax.rsqrt(jnp.mean(xf * xf, axis=-1, keepdims=True) + EPS)
    return (y * g.astype(jnp.float32)).astype(x.dtype)


def _layernorm(x, g, b):
    xf = x.astype(jnp.float32)
    mu = jnp.mean(xf, axis=-1, keepdims=True)
    var = jnp.mean(jnp.square(xf - mu), axis=-1, keepdims=True)
    y = (xf - mu) * lax.rsqrt(var + EPS)
    return (y * g.astype(jnp.float32) + b.astype(jnp.float32)).astype(x.dtype)


def _group_rmsnorm(x, g):
    shp = x.shape
    xf = x.astype(jnp.float32).reshape(shp[:-1] + (SSM_GROUPS, shp[-1] // SSM_GROUPS))
    y = xf * lax.rsqrt(jnp.mean(xf * xf, axis=-1, keepdims=True) + EPS)
    return (y.reshape(shp) * g.astype(jnp.float32)).astype(x.dtype)


def _causal_dwconv(prev, x, w, b):
    width, ch = w.shape
    full = jnp.concatenate([prev.astype(x.dtype), x], axis=1)
    y = lax.conv_general_dilated(full, w[:, None, :].astype(x.dtype), window_strides=(1,),
                                 padding='VALID', dimension_numbers=('NWC', 'WIO', 'NWC'),
                                 feature_group_count=ch)
    return y + b.astype(y.dtype), full[:, full.shape[1] - (width - 1):]


def _ssd(x, dt, A, Bm, Cm, D, h0):
    f32 = jnp.float32
    b, L, H, P = x.shape
    G, N = Bm.shape[2], Bm.shape[3]
    E = H // G
    pad = (-L) % CHUNK
    fpad = lambda t: jnp.pad(t.astype(f32), [(0, 0), (pad, 0)] + [(0, 0)] * (t.ndim - 2))
    nc = (L + pad) // CHUNK
    xs = fpad(x).reshape(b, nc, CHUNK, G, E, P)
    dts = fpad(dt).reshape(b, nc, CHUNK, G, E)
    Bs = fpad(Bm).reshape(b, nc, CHUNK, G, N)
    Cs = fpad(Cm).reshape(b, nc, CHUNK, G, N)
    a_cs = jnp.cumsum(dts * A.astype(f32).reshape(G, E), axis=2)
    xdt = xs * dts[..., None]
    diff = a_cs[:, :, :, None] - a_cs[:, :, None, :]
    mask = jnp.tril(jnp.ones((CHUNK, CHUNK), bool))[:, :, None, None]
    decay = jnp.where(mask, jnp.exp(jnp.where(mask, diff, 0.0)), 0.0)
    cb = jnp.einsum('bclgn,bcsgn->bclsg', Cs, Bs)
    y_diag = jnp.einsum('bclsg,bclsge,bcsgep->bclgep', cb, decay, xdt)
    decay_to_end = jnp.exp(a_cs[:, :, -1:] - a_cs)
    states = jnp.einsum('bclgn,bclge,bclgep->bcgepn', Bs, decay_to_end, xdt)
    chunk_decay = jnp.exp(a_cs[:, :, -1])

    def step(h, inp):
        s, d = inp
        return h * d[..., None, None] + s, h

    h_init = h0.astype(f32).reshape(b, G, E, P, N)
    h_final, h_in = lax.scan(step, h_init, (jnp.swapaxes(states, 0, 1), jnp.swapaxes(chunk_decay, 0, 1)))
    h_in = jnp.swapaxes(h_in, 0, 1)
    y_off = jnp.einsum('bclgn,bcgepn,bclge->bclgep', Cs, h_in, jnp.exp(a_cs))
    y = y_diag + y_off + xs * D.astype(f32).reshape(G, E)[..., None]
    y = y.reshape(b, nc * CHUNK, H, P)[:, pad:]
    return y.astype(x.dtype), h_final.reshape(b, H, P, N)


def _layer(h, conf_prev, xbc_prev, ssm_prev, ffn_prev, p):
    (g_mix, w_in, conf_w, conf_b, ln_g, ln_b, sconv_w, sconv_b, dt_bias, a_log, d_skip,
     snorm_g, w_out, g_ffn, w_up, fconv_w, fconv_b, w_down) = p
    b, L, _ = h.shape
    u = _rmsnorm(h, g_mix)
    proj = u @ w_in
    conf_in = proj[..., :OFF_Z]
    z = proj[..., OFF_Z:OFF_XBC]
    xbc = proj[..., OFF_XBC:OFF_DT]
    dt_raw = proj[..., OFF_DT:]
    glu = conf_in[..., :D_CONV_GRP] * jax.nn.sigmoid(conf_in[..., D_CONV_GRP:])
    c, conf_buf = _causal_dwconv(conf_prev, glu, conf_w, conf_b)
    c = jax.nn.silu(_layernorm(c, ln_g, ln_b))
    xbc_c, xbc_buf = _causal_dwconv(xbc_prev, xbc, sconv_w, sconv_b)
    xbc_c = jax.nn.silu(xbc_c)
    xs = xbc_c[..., :D_SSM].reshape(b, L, SSM_HEADS, SSM_HEAD_DIM)
    Bm = xbc_c[..., D_SSM:D_SSM + GN].reshape(b, L, SSM_GROUPS, SSM_STATE)
    Cm = xbc_c[..., D_SSM + GN:].reshape(b, L, SSM_GROUPS, SSM_STATE)
    dt = jax.nn.softplus(dt_raw.astype(jnp.float32) + dt_bias.astype(jnp.float32))
    A = -jnp.exp(a_log.astype(jnp.float32))
    y, h_ssm = _ssd(xs, dt, A, Bm, Cm, d_skip, ssm_prev)
    y = _group_rmsnorm(y.reshape(b, L, D_SSM) * jax.nn.silu(z), snorm_g)
    h = h + jnp.concatenate([c, y], axis=-1) @ w_out
    up, ffn_buf = _causal_dwconv(ffn_prev, _rmsnorm(h, g_ffn) @ w_up, fconv_w, fconv_b)
    h = h + (jax.nn.silu(up[..., :D_FF]) * up[..., D_FF:]) @ w_down
    return h, conf_buf, xbc_buf, h_ssm.astype(ssm_prev.dtype), ffn_buf


def setup_inputs(seed: int = 0) -> dict:
    key = jax.random.key(seed)
    ks = jax.random.split(key, 32)
    f32 = jnp.float32
    nrm = lambda k, shape, s: jax.random.normal(k, shape, f32) * s
    dt0 = jnp.exp(jax.random.uniform(ks[14], (DEPTH, SSM_HEADS), f32, np.log(1e-3), np.log(1e-1)))
    return {
        "x_prompt": nrm(ks[0], (BATCH, SEQ, D_MODEL), 1.0),
        "x_sample": nrm(ks[1], (DEC_BATCH, DEC_SEQ, D_MODEL), 1.0),
        "state_conf_conv": nrm(ks[2], (DEPTH, DEC_BATCH, CONF_KERNEL - 1, D_CONV_GRP), 0.5),
        "state_xbc_conv": nrm(ks[3], (DEPTH, DEC_BATCH, SSM_CONV - 1, XBC_DIM), 1.0),
        "state_ssm": nrm(ks[4], (DEPTH, DEC_BATCH, SSM_HEADS, SSM_HEAD_DIM, SSM_STATE), 0.5),
        "state_ffn_conv": nrm(ks[5], (DEPTH, DEC_BATCH, FFN_KERNEL - 1, 2 * D_FF), 1.0),
        "meta_tokens": nrm(ks[6], (N_META, D_MODEL), 1.0),
        "norm_mix_g": 1.0 + nrm(ks[7], (DEPTH, D_MODEL), 0.02),
        "w_in": nrm(ks[8], (DEPTH, D_MODEL, IN_PROJ_DIM), D_MODEL ** -0.5),
        "conf_conv_w": nrm(ks[9], (DEPTH, CONF_KERNEL, D_CONV_GRP), CONF_KERNEL ** -0.5),
        "conf_conv_b": nrm(ks[10], (DEPTH, D_CONV_GRP), 0.02),
        "conf_ln_g": 1.0 + nrm(ks[11], (DEPTH, D_CONV_GRP), 0.02),
        "conf_ln_b": nrm(ks[12], (DEPTH, D_CONV_GRP), 0.02),
        "ssm_conv_w": nrm(ks[13], (DEPTH, SSM_CONV, XBC_DIM), SSM_CONV ** -0.5),
        "ssm_conv_b": nrm(ks[15], (DEPTH, XBC_DIM), 0.02),
        "dt_bias": dt0 + jnp.log(-jnp.expm1(-dt0)),
        "a_log": jnp.log(jax.random.uniform(ks[16], (DEPTH, SSM_HEADS), f32, 1.0, 16.0)),
        "d_skip": 1.0 + nrm(ks[17], (DEPTH, SSM_HEADS), 0.1),
        "ssm_norm_g": 1.0 + nrm(ks[18], (DEPTH, D_SSM), 0.02),
        "w_out": nrm(ks[19], (DEPTH, D_MIX, D_MODEL), D_MIX ** -0.5),
        "norm_ffn_g": 1.0 + nrm(ks[20], (DEPTH, D_MODEL), 0.02),
        "w_up": nrm(ks[21], (DEPTH, D_MODEL, 2 * D_FF), D_MODEL ** -0.5),
        "ffn_conv_w": nrm(ks[22], (DEPTH, FFN_KERNEL, 2 * D_FF), FFN_KERNEL ** -0.5),
        "ffn_conv_b": nrm(ks[23], (DEPTH, 2 * D_FF), 0.02),
        "w_down": nrm(ks[24], (DEPTH, D_FF, D_MODEL), D_FF ** -0.5),
        "norm_final_g": 1.0 + nrm(ks[25], (D_MODEL,), 0.02),
    }


def reference(x_prompt, x_sample, state_conf_conv, state_xbc_conv, state_ssm, state_ffn_conv,
              meta_tokens, norm_mix_g, w_in, conf_conv_w, conf_conv_b, conf_ln_g, conf_ln_b,
              ssm_conv_w, ssm_conv_b, dt_bias, a_log, d_skip, ssm_norm_g, w_out, norm_ffn_g,
              w_up, ffn_conv_w, ffn_conv_b, w_down, norm_final_g):
    bp = x_prompt.shape[0]
    dt_ = x_prompt.dtype
    hp = jnp.concatenate([jnp.broadcast_to(meta_tokens.astype(dt_)[None], (bp, N_META, D_MODEL)), x_prompt], axis=1)
    hs = x_sample
    pc, px, ps, pf = [], [], [], []
    sc, sx, ss, sf = [], [], [], []
    for l in range(DEPTH):
        p = (norm_mix_g[l], w_in[l], conf_conv_w[l], conf_conv_b[l], conf_ln_g[l], conf_ln_b[l],
             ssm_conv_w[l], ssm_conv_b[l], dt_bias[l], a_log[l], d_skip[l], ssm_norm_g[l], w_out[l],
             norm_ffn_g[l], w_up[l], ffn_conv_w[l], ffn_conv_b[l], w_down[l])
        hp, c_b, x_b, s_b, f_b = _layer(
            hp,
            jnp.zeros((bp, CONF_KERNEL - 1, D_CONV_GRP), dt_),
            jnp.zeros((bp, SSM_CONV - 1, XBC_DIM), dt_),
            jnp.zeros((bp, SSM_HEADS, SSM_HEAD_DIM, SSM_STATE), jnp.float32),
            jnp.zeros((bp, FFN_KERNEL - 1, 2 * D_FF), dt_), p)
        pc.append(c_b); px.append(x_b); ps.append(s_b); pf.append(f_b)
        hs, c_b, x_b, s_b, f_b = _layer(hs, state_conf_conv[l], state_xbc_conv[l], state_ssm[l],
                                        state_ffn_conv[l], p)
        sc.append(c_b); sx.append(x_b); ss.append(s_b); sf.append(f_b)
    y_prompt = _rmsnorm(hp, norm_final_g)[:, N_META:]
    y_sample = _rmsnorm(hs, norm_final_g)
    return (y_prompt, y_sample,
            jnp.stack(pc), jnp.stack(px), jnp.stack(ps), jnp.stack(pf),
            jnp.stack(sc), jnp.stack(sx), jnp.stack(ss), jnp.stack(sf))
```

```python
import functools

import jax
import jax.numpy as jnp
from jax import lax
from jax.experimental import pallas as pl
from jax.experimental.pallas import tpu as pltpu

F32 = jnp.float32
BF16 = jnp.bfloat16

D_MODEL = 1024
N_META = 16
D_CONV = 1024
D_SSM = 1024
HEAD_DIM = 64
N_HEADS = D_SSM // HEAD_DIM
N_GROUPS = 2
HEADS_PER_GROUP = N_HEADS // N_GROUPS
D_STATE = 128
GN = N_GROUPS * D_STATE
GROUP_W = D_SSM // N_GROUPS
XBC = D_SSM + 2 * GN
D_FF = 2816
CONF_K = 31
SSM_K = 4
FFN_K = 3
OFF_GATE = D_CONV
OFF_Z = 2 * D_CONV
OFF_XBC = OFF_Z + D_SSM
OFF_DT = OFF_XBC + XBC
LANES = 128
SUBLANES = 8
PROJ_W = OFF_DT + LANES
EPS = 1e-5

SSD_CHUNK = 128
SAMPLE_ROWS = 8
SAMPLE_SEQS = 8
MIB = 1024 * 1024


def _sigmoid(x):
    return 1.0 / (1.0 + jnp.exp(-x))


def _rms(x, g):
    return x * lax.rsqrt(jnp.mean(x * x, axis=-1, keepdims=True) + EPS) * g


def _mm(a, b):
    return jnp.dot(a, b, preferred_element_type=F32)


def _mm_nt(a, b):
    return lax.dot_general(a, b, (((1,), (1,)), ((), ())), preferred_element_type=F32)


def _const_spec(shape):
    zeros = (0,) * len(shape)
    return pl.BlockSpec(shape, lambda *_: zeros, pipeline_mode=pl.Buffered(1))


def _params(vmem_mib, ndims=1):
    return pltpu.CompilerParams(dimension_semantics=("arbitrary",) * ndims,
                                vmem_limit_bytes=vmem_mib * MIB)


def _in_proj_body(x_ref, g_ref, w_ref, o_ref):
    u = _rms(x_ref[...], g_ref[...])
    o_ref[...] = _mm(u.astype(BF16), w_ref[...])


def _in_proj(x, g, w, row_tile):
    rows = x.shape[0]
    return pl.pallas_call(
        _in_proj_body,
        out_shape=jax.ShapeDtypeStruct((rows, PROJ_W), F32),
        grid=(rows // row_tile,),
        in_specs=[pl.BlockSpec((row_tile, D_MODEL), lambda i: (i, 0)),
                  _const_spec((1, D_MODEL)),
                  _const_spec((D_MODEL, PROJ_W))],
        out_specs=pl.BlockSpec((row_tile, PROJ_W), lambda i: (i, 0)),
        compiler_params=_params(44),
        name="in_proj",
    )(x, g, w)


def _out_up_body(cy_ref, h_ref, wo_ref, g_ref, wu_ref, h1_ref, up_ref):
    h1 = h_ref[...] + _mm(cy_ref[...], wo_ref[...])
    h1_ref[...] = h1
    up_ref[...] = _mm(_rms(h1, g_ref[...]).astype(BF16), wu_ref[...])


def _out_up(cy, h, w_out, g, w_up, row_tile):
    rows = h.shape[0]
    return pl.pallas_call(
        _out_up_body,
        out_shape=(jax.ShapeDtypeStruct((rows, D_MODEL), F32),
                   jax.ShapeDtypeStruct((rows, 2 * D_FF), F32)),
        grid=(rows // row_tile,),
        in_specs=[pl.BlockSpec((row_tile, 2 * D_CONV), lambda i: (i, 0)),
                  pl.BlockSpec((row_tile, D_MODEL), lambda i: (i, 0)),
                  _const_spec((2 * D_CONV, D_MODEL)),
                  _const_spec((1, D_MODEL)),
                  _const_spec((D_MODEL, 2 * D_FF))],
        out_specs=(pl.BlockSpec((row_tile, D_MODEL), lambda i: (i, 0)),
                   pl.BlockSpec((row_tile, 2 * D_FF), lambda i: (i, 0))),
        compiler_params=_params(52),
        name="out_up",
    )(cy, h, w_out, g, w_up)


def _split3(v):
    v1 = v.astype(BF16)
    r = v - v1.astype(F32)
    v2 = r.astype(BF16)
    v3 = (r - v2.astype(F32)).astype(BF16)
    return v1, v2, v3


def _ssd_chunk(xs, bm, cm, dt, a, z, rows_per_seq, e2, d_exp, snorm_g, h_get, h_set):
    tile = xs.shape[0]
    nseq = tile // rows_per_seq
    shift = rows_per_seq.bit_length() - 1
    row = lax.broadcasted_iota(jnp.int32, (tile, tile), 0)
    col = lax.broadcasted_iota(jnp.int32, (tile, tile), 1)
    causal = col <= row
    if nseq > 1:
        same = lax.shift_right_logical(row, shift) == lax.shift_right_logical(col, shift)
        mask = jnp.logical_and(same, causal)
        bd = jnp.where(same, 1.0, 0.0).astype(BF16)
    else:
        mask = causal
        bd = jnp.ones((tile, tile), BF16)
    tri = jnp.where(mask, 1.0, 0.0).astype(BF16)

    a1, a2, a3 = _split3(a)
    acs = _mm(tri, a1) + _mm(tri, a2) + _mm(tri, a3)
    tot = _mm(bd, a1) + _mm(bd, a2) + _mm(bd, a3)
    acs_t = acs.T

    def expand(q):
        q1 = q.astype(BF16)
        q2 = (q - q1.astype(F32)).astype(BF16)
        return _mm(jnp.concatenate([q1, q2], axis=1), e2)

    xdt = xs * expand(dt)
    xw_t = (xs * expand(dt * jnp.exp(tot - acs))).T.astype(BF16)
    seg_e = expand(jnp.exp(acs))

    lane = lax.broadcasted_iota(jnp.int32, (tile, LANES), 1)
    low_half = lane < HEAD_DIM
    rowv = lax.broadcasted_iota(jnp.int32, (tile, D_STATE), 0)
    seq_of_row = lax.shift_right_logical(rowv, shift)

    ydiag = []
    yoff = []
    for g in range(N_GROUPS):
        cg = cm[:, g * D_STATE:(g + 1) * D_STATE]
        bg = bm[:, g * D_STATE:(g + 1) * D_STATE]
        cgb = cg.astype(BF16)
        bgb = bg.astype(BF16)
        cb = _mm_nt(cgb, bgb)
        for pair in range(HEADS_PER_GROUP // 2):
            blk = g * (HEADS_PER_GROUP // 2) + pair
            xb = xdt[:, blk * LANES:(blk + 1) * LANES]
            acc = None
            for half in range(2):
                h = 2 * blk + half
                diff = acs[:, h:h + 1] - acs_t[h:h + 1, :]
                decay = jnp.where(mask, jnp.exp(jnp.where(mask, diff, 0.0)), 0.0)
                xh = jnp.where(low_half if half == 0 else jnp.logical_not(low_half), xb, 0.0)
                term = _mm((cb * decay).astype(BF16), xh.astype(BF16))
                acc = term if acc is None else acc + term
            ydiag.append(acc)

        acc = None
        for b in range(nseq):
            hprev = h_get(b, g)
            if nseq > 1:
                sel = seq_of_row == b
                c_b = jnp.where(sel, cg, 0.0).astype(BF16)
                b_b = jnp.where(sel, bg, 0.0).astype(BF16)
            else:
                c_b, b_b = cgb, bgb
            term = _mm_nt(c_b, hprev.astype(BF16))
            acc = term if acc is None else acc + term
            dec = jnp.exp(tot[b * rows_per_seq:b * rows_per_seq + 1, :])
            scaled = [hprev[e * HEAD_DIM:(e + 1) * HEAD_DIM, :]
                      * dec[:, g * HEADS_PER_GROUP + e:g * HEADS_PER_GROUP + e + 1]
                      for e in range(HEADS_PER_GROUP)]
            upd = _mm(xw_t[g * GROUP_W:(g + 1) * GROUP_W, :], b_b)
            h_set(b, g, jnp.concatenate(scaled, axis=0) + upd)
        yoff.append(acc)

    y = (jnp.concatenate(ydiag, axis=1) + jnp.concatenate(yoff, axis=1) * seg_e + xs * d_exp)
    yz = y * (z * _sigmoid(z))
    outs = []
    for g in range(N_GROUPS):
        blk = yz[:, g * GROUP_W:(g + 1) * GROUP_W]
        ms = jnp.mean(blk * blk, axis=-1, keepdims=True)
        outs.append(blk * lax.rsqrt(ms + EPS) * snorm_g[:, g * GROUP_W:(g + 1) * GROUP_W])
    return jnp.concatenate(outs, axis=1)


def _dt_and_a(dt_raw, dtb, alog, pad_rows, valid_rows_per_tile=None):
    v = dt_raw + dtb
    dt = jnp.maximum(v, 0.0) + jnp.log1p(jnp.exp(-jnp.abs(v)))
    lane = lax.broadcasted_iota(jnp.int32, dt.shape, 1)
    keep = lane < N_HEADS
    row = lax.broadcasted_iota(jnp.int32, dt.shape, 0)
    if pad_rows:
        keep = jnp.logical_and(row >= pad_rows, keep)
    if valid_rows_per_tile is not None:
        keep = jnp.logical_and((row & (SAMPLE_ROWS - 1)) < valid_rows_per_tile, keep)
    dt = jnp.where(keep, dt, 0.0)
    return dt, dt * (-jnp.exp(alog))


def _layernorm_silu(cv, g, b):
    mu = jnp.mean(cv, axis=-1, keepdims=True)
    d = cv - mu
    var = jnp.mean(d * d, axis=-1, keepdims=True)
    cn = d * lax.rsqrt(var + EPS) * g + b
    return cn * _sigmoid(cn)


CONV_ROWS = 32
CONV_COLS = 512
CONF_PAD = 32
XBC_PAD = 8


def _mix_p_body(pad_rows, proj_ref, conf0_ref, xbc0_ref, ssm0_ref, cw_ref, cb_ref, lng_ref,
                lnb_ref, sw_ref, sb_ref, dtb_ref, alog_ref, dexp_ref, sng_ref, e2_ref,
                cy_ref, conf_out_ref, xbc_out_ref, ssm_out_ref, gbuf, xbuf, hst, cbuf):
    c = pl.program_id(1)
    last = pl.num_programs(1) - 1
    tile = proj_ref.shape[0]
    conf_lo = CONF_PAD - (CONF_K - 1)
    xbc_lo = XBC_PAD - (SSM_K - 1)

    @pl.when(c == 0)
    def _():
        gbuf[conf_lo:CONF_PAD, :] = conf0_ref[0]
        xbuf[xbc_lo:XBC_PAD, :] = xbc0_ref[0]
        hst[...] = ssm0_ref[0].reshape(N_HEADS * HEAD_DIM, D_STATE)

    for cb_i in range(D_CONV // CONV_COLS):
        cols = slice(cb_i * CONV_COLS, (cb_i + 1) * CONV_COLS)
        gate = proj_ref[:, OFF_GATE + cb_i * CONV_COLS:OFF_GATE + (cb_i + 1) * CONV_COLS]
        gbuf[CONF_PAD:CONF_PAD + tile, cols] = proj_ref[:, cols] * _sigmoid(gate)
    for rb in range(tile // CONV_ROWS):
        for cb_i in range(D_CONV // CONV_COLS):
            cols = slice(cb_i * CONV_COLS, (cb_i + 1) * CONV_COLS)
            acc = jnp.broadcast_to(cb_ref[:, cols], (CONV_ROWS, CONV_COLS))
            for k in range(CONF_K):
                r0 = rb * CONV_ROWS + conf_lo + k
                acc = acc + cw_ref[k:k + 1, cols] * gbuf[r0:r0 + CONV_ROWS, cols]
            cbuf[rb * CONV_ROWS:(rb + 1) * CONV_ROWS, cols] = acc
    cy_ref[:, 0:D_CONV] = _layernorm_silu(cbuf[...], lng_ref[...], lnb_ref[...]).astype(BF16)

    @pl.when(c == last)
    def _():
        conf_out_ref[0] = gbuf[tile + conf_lo:tile + CONF_PAD, :]

    gbuf[0:CONF_PAD, :] = gbuf[tile:tile + CONF_PAD, :]

    xbuf[XBC_PAD:XBC_PAD + tile, :] = proj_ref[:, OFF_XBC:OFF_DT]
    xv = jnp.broadcast_to(sb_ref[...], (tile, XBC))
    for k in range(SSM_K):
        xv = xv + sw_ref[k:k + 1, :] * xbuf[xbc_lo + k:xbc_lo + k + tile, :]
    xv = xv * _sigmoid(xv)

    @pl.when(c == last)
    def _():
        xbc_out_ref[0] = xbuf[tile + xbc_lo:tile + XBC_PAD, :]

    xbuf[0:XBC_PAD, :] = xbuf[tile:tile + XBC_PAD, :]

    dt, a = _dt_and_a(proj_ref[:, OFF_DT:PROJ_W], dtb_ref[...], alog_ref[...], pad_rows)

    def h_get(b, g):
        return hst[g * GROUP_W:(g + 1) * GROUP_W, :]

    def h_set(b, g, val):
        hst[g * GROUP_W:(g + 1) * GROUP_W, :] = val

    for j in range(tile // SSD_CHUNK):
        rs = slice(j * SSD_CHUNK, (j + 1) * SSD_CHUNK)
        out = _ssd_chunk(xv[rs, 0:D_SSM], xv[rs, D_SSM:D_SSM + GN], xv[rs, D_SSM + GN:XBC],
                         dt[rs], a[rs], proj_ref[rs, OFF_Z:OFF_XBC], SSD_CHUNK, e2_ref[...],
                         dexp_ref[...], sng_ref[...], h_get, h_set)
        cy_ref[rs, D_CONV:2 * D_CONV] = out.astype(BF16)

    @pl.when(c == last)
    def _():
        ssm_out_ref[0] = hst[...].reshape(N_HEADS, HEAD_DIM, D_STATE)


def _mix_p(proj, conf0, xbc0, ssm0, p, nbatch, tile, pad_rows):
    rows = proj.shape[0]
    nchunk = rows // (nbatch * tile)
    row_map = lambda b, c: (b * nchunk + c, 0)
    state_specs = lambda shp: pl.BlockSpec((1,) + shp, lambda b, c: (b,) + (0,) * len(shp))
    small = [p["conf_w"], p["conf_b"], p["ln_g"], p["ln_b"], p["sconv_w"], p["sconv_b"],
             p["dt_bias"], p["a_log"], p["d_exp"], p["snorm_g"], p["e2"]]
    return pl.pallas_call(
        functools.partial(_mix_p_body, pad_rows),
        out_shape=(jax.ShapeDtypeStruct((rows, 2 * D_CONV), BF16),
                   jax.ShapeDtypeStruct((nbatch, CONF_K - 1, D_CONV), F32),
                   jax.ShapeDtypeStruct((nbatch, SSM_K - 1, XBC), F32),
                   jax.ShapeDtypeStruct((nbatch, N_HEADS, HEAD_DIM, D_STATE), F32)),
        grid=(nbatch, nchunk),
        in_specs=[pl.BlockSpec((tile, PROJ_W), row_map),
                  _const_spec((1, CONF_K - 1, D_CONV)),
                  _const_spec((1, SSM_K - 1, XBC)),
                  _const_spec((1, N_HEADS, HEAD_DIM, D_STATE))]
                 + [_const_spec(s.shape) for s in small],
        out_specs=(pl.BlockSpec((tile, 2 * D_CONV), row_map),
                   state_specs((CONF_K - 1, D_CONV)),
                   state_specs((SSM_K - 1, XBC)),
                   state_specs((N_HEADS, HEAD_DIM, D_STATE))),
        scratch_shapes=[pltpu.VMEM((tile + CONF_PAD, D_CONV), F32),
                        pltpu.VMEM((tile + XBC_PAD, XBC), F32),
                        pltpu.VMEM((N_HEADS * HEAD_DIM, D_STATE), F32),
                        pltpu.VMEM((tile, D_CONV), F32)],
        compiler_params=_params(48, 2),
        name="mix_prompt",
    )(proj, conf0, xbc0, ssm0, *small)


def _mix_s_body(proj_ref, conf_ref, xbc_ref, ssm_ref, cw_ref, cb_ref, lng_ref, lnb_ref, sw_ref,
                sb_ref, dtb_ref, alog_ref, dexp_ref, sng_ref, e2_ref,
                cy_ref, conf_out_ref, xbc_out_ref, ssm_out_ref, fbuf, xfb):
    nseq = conf_ref.shape[0]
    rows = nseq * SAMPLE_ROWS
    new_rows = proj_ref.shape[0] // nseq // 2
    conf_lo = CONF_PAD - (CONF_K - 1)
    xbc_lo = XBC_PAD - (SSM_K - 1)

    glu = proj_ref[:, 0:D_CONV] * _sigmoid(proj_ref[:, OFF_GATE:OFF_Z])
    fbuf[:, conf_lo:CONF_PAD, :] = conf_ref[...]
    fbuf[:, CONF_PAD:CONF_PAD + SAMPLE_ROWS, :] = glu.reshape(nseq, SAMPLE_ROWS, D_CONV)
    acc = jnp.broadcast_to(cb_ref[...].reshape(1, 1, D_CONV), (nseq, SAMPLE_ROWS, D_CONV))
    for k in range(CONF_K):
        w = cw_ref[k:k + 1, :].reshape(1, 1, D_CONV)
        acc = acc + w * fbuf[:, conf_lo + k:conf_lo + k + SAMPLE_ROWS, :]
    conf_out_ref[...] = fbuf[:, conf_lo + new_rows:CONF_PAD + new_rows, :]
    cy_ref[:, 0:D_CONV] = _layernorm_silu(acc.reshape(rows, D_CONV), lng_ref[...],
                                          lnb_ref[...]).astype(BF16)

    xfb[:, xbc_lo:XBC_PAD, :] = xbc_ref[...]
    xfb[:, XBC_PAD:XBC_PAD + SAMPLE_ROWS, :] = proj_ref[:, OFF_XBC:OFF_DT].reshape(
        nseq, SAMPLE_ROWS, XBC)
    xv = jnp.broadcast_to(sb_ref[...].reshape(1, 1, XBC), (nseq, SAMPLE_ROWS, XBC))
    for k in range(SSM_K):
        w = sw_ref[k:k + 1, :].reshape(1, 1, XBC)
        xv = xv + w * xfb[:, xbc_lo + k:xbc_lo + k + SAMPLE_ROWS, :]
    xbc_out_ref[...] = xfb[:, xbc_lo + new_rows:XBC_PAD + new_rows, :]
    xv = xv.reshape(rows, XBC)
    xv = xv * _sigmoid(xv)

    dt, a = _dt_and_a(proj_ref[:, OFF_DT:PROJ_W], dtb_ref[...], alog_ref[...], 0, new_rows)

    def h_get(b, g):
        return ssm_ref[b, g * HEADS_PER_GROUP:(g + 1) * HEADS_PER_GROUP].reshape(GROUP_W, D_STATE)

    def h_set(b, g, val):
        ssm_out_ref[b, g * HEADS_PER_GROUP:(g + 1) * HEADS_PER_GROUP] = val.reshape(
            HEADS_PER_GROUP, HEAD_DIM, D_STATE)

    out = _ssd_chunk(xv[:, 0:D_SSM], xv[:, D_SSM:D_SSM + GN], xv[:, D_SSM + GN:XBC], dt, a,
                     proj_ref[:, OFF_Z:OFF_XBC], SAMPLE_ROWS, e2_ref[...], dexp_ref[...],
                     sng_ref[...], h_get, h_set)
    cy_ref[:, D_CONV:2 * D_CONV] = out.astype(BF16)


def _mix_s(proj, conf, xbc, ssm, p):
    nbatch = conf.shape[0]
    rows = SAMPLE_SEQS * SAMPLE_ROWS
    small = [p["conf_w"], p["conf_b"], p["ln_g"], p["ln_b"], p["sconv_w"], p["sconv_b"],
             p["dt_bias"], p["a_log"], p["d_exp"], p["snorm_g"], p["e2"]]
    blk3 = lambda shp: pl.BlockSpec((SAMPLE_SEQS,) + shp, lambda i: (i,) + (0,) * len(shp))
    return pl.pallas_call(
        _mix_s_body,
        out_shape=(jax.ShapeDtypeStruct((nbatch * SAMPLE_ROWS, 2 * D_CONV), BF16),
                   jax.ShapeDtypeStruct(conf.shape, F32),
                   jax.ShapeDtypeStruct(xbc.shape, F32),
                   jax.ShapeDtypeStruct(ssm.shape, F32)),
        grid=(nbatch // SAMPLE_SEQS,),
        in_specs=[pl.BlockSpec((rows, PROJ_W), lambda i: (i, 0)),
                  blk3((CONF_K - 1, D_CONV)), blk3((SSM_K - 1, XBC)),
                  blk3((N_HEADS, HEAD_DIM, D_STATE))]
                 + [_const_spec(s.shape) for s in small],
        out_specs=(pl.BlockSpec((rows, 2 * D_CONV), lambda i: (i, 0)),
                   blk3((CONF_K - 1, D_CONV)), blk3((SSM_K - 1, XBC)),
                   blk3((N_HEADS, HEAD_DIM, D_STATE))),
        scratch_shapes=[pltpu.VMEM((SAMPLE_SEQS, CONF_PAD + SAMPLE_ROWS, D_CONV), F32),
                        pltpu.VMEM((SAMPLE_SEQS, XBC_PAD + SAMPLE_ROWS, XBC), F32)],
        compiler_params=_params(40),
        name="mix_sample",
    )(proj, conf, xbc, ssm, *small)


FFN_COLS = 256
FFN_PAD = 8


def _ffn_p_body(up_ref, h1_ref, tail0_ref, cw_ref, cb_ref, wd_ref, gf_ref,
                y_ref, tail_out_ref, ubuf, act):
    c = pl.program_id(1)
    tile = up_ref.shape[0]
    lo = FFN_PAD - (FFN_K - 1)

    @pl.when(c == 0)
    def _():
        ubuf[0:FFN_PAD, :] = tail0_ref[...]

    ubuf[FFN_PAD:FFN_PAD + tile, :] = up_ref[...]

    def conv(cols):
        out = jnp.broadcast_to(cb_ref[:, cols], (tile, FFN_COLS))
        for k in range(FFN_K):
            out = out + cw_ref[k:k + 1, cols] * ubuf[lo + k:lo + k + tile, cols]
        return out

    for j in range(D_FF // FFN_COLS):
        gate = conv(slice(j * FFN_COLS, (j + 1) * FFN_COLS))
        lin = conv(slice(D_FF + j * FFN_COLS, D_FF + (j + 1) * FFN_COLS))
        act[:, j * FFN_COLS:(j + 1) * FFN_COLS] = (gate * _sigmoid(gate) * lin).astype(BF16)

    ubuf[0:FFN_PAD, :] = ubuf[tile:tile + FFN_PAD, :]

    @pl.when(c == pl.num_programs(1) - 1)
    def _():
        tail_out_ref[0] = ubuf[lo:FFN_PAD, :]

    y_ref[...] = _rms(h1_ref[...] + _mm(act[...], wd_ref[...]), gf_ref[...])


def _ffn_p(up, h1, tail0, p, nbatch, tile):
    rows = up.shape[0]
    nchunk = rows // (nbatch * tile)
    row_map = lambda b, c: (b * nchunk + c, 0)
    return pl.pallas_call(
        _ffn_p_body,
        out_shape=(jax.ShapeDtypeStruct((rows, D_MODEL), F32),
                   jax.ShapeDtypeStruct((nbatch, FFN_K - 1, 2 * D_FF), F32)),
        grid=(nbatch, nchunk),
        in_specs=[pl.BlockSpec((tile, 2 * D_FF), row_map),
                  pl.BlockSpec((tile, D_MODEL), row_map),
                  _const_spec((FFN_PAD, 2 * D_FF)),
                  _const_spec((FFN_K, 2 * D_FF)),
                  _const_spec((1, 2 * D_FF)),
                  _const_spec((D_FF, D_MODEL)),
                  _const_spec((1, D_MODEL))],
        out_specs=(pl.BlockSpec((tile, D_MODEL), row_map),
                   pl.BlockSpec((1, FFN_K - 1, 2 * D_FF), lambda b, c: (b, 0, 0))),
        scratch_shapes=[pltpu.VMEM((tile + FFN_PAD, 2 * D_FF), F32),
                        pltpu.VMEM((tile, D_FF), BF16)],
        compiler_params=_params(48, 2),
        name="ffn_prompt",
    )(up, h1, tail0, p["fconv_w"], p["fconv_b"], p["w_down"], p["g_final"])


def _ffn_s_body(up_ref, h1_ref, st_ref, cw_ref, cb_ref, wd_ref, gf_ref,
                y_ref, st_out_ref, ubuf, act):
    nseq = st_ref.shape[0]
    rows = nseq * SAMPLE_ROWS
    new_rows = SAMPLE_ROWS // 2
    lo = FFN_PAD - (FFN_K - 1)
    ubuf[:, lo:FFN_PAD, :] = st_ref[...]
    ubuf[:, FFN_PAD:FFN_PAD + SAMPLE_ROWS, :] = up_ref[...].reshape(nseq, SAMPLE_ROWS, 2 * D_FF)

    def conv(cols):
        out = jnp.broadcast_to(cb_ref[:, cols].reshape(1, 1, FFN_COLS),
                               (nseq, SAMPLE_ROWS, FFN_COLS))
        for k in range(FFN_K):
            w = cw_ref[k:k + 1, cols].reshape(1, 1, FFN_COLS)
            out = out + w * ubuf[:, lo + k:lo + k + SAMPLE_ROWS, cols]
        return out.reshape(rows, FFN_COLS)

    for j in range(D_FF // FFN_COLS):
        gate = conv(slice(j * FFN_COLS, (j + 1) * FFN_COLS))
        lin = conv(slice(D_FF + j * FFN_COLS, D_FF + (j + 1) * FFN_COLS))
        act[:, j * FFN_COLS:(j + 1) * FFN_COLS] = (gate * _sigmoid(gate) * lin).astype(BF16)

    st_out_ref[...] = ubuf[:, lo + new_rows:FFN_PAD + new_rows, :]
    y_ref[...] = _rms(h1_ref[...] + _mm(act[...], wd_ref[...]), gf_ref[...])


def _ffn_s(up, h1, st, p):
    nbatch = st.shape[0]
    rows = SAMPLE_SEQS * SAMPLE_ROWS
    return pl.pallas_call(
        _ffn_s_body,
        out_shape=(jax.ShapeDtypeStruct((nbatch * SAMPLE_ROWS, D_MODEL), F32),
                   jax.ShapeDtypeStruct(st.shape, F32)),
        grid=(nbatch // SAMPLE_SEQS,),
        in_specs=[pl.BlockSpec((rows, 2 * D_FF), lambda i: (i, 0)),
                  pl.BlockSpec((rows, D_MODEL), lambda i: (i, 0)),
                  pl.BlockSpec((SAMPLE_SEQS, FFN_K - 1, 2 * D_FF), lambda i: (i, 0, 0)),
                  _const_spec((FFN_K, 2 * D_FF)),
                  _const_spec((1, 2 * D_FF)),
                  _const_spec((D_FF, D_MODEL)),
                  _const_spec((1, D_MODEL))],
        out_specs=(pl.BlockSpec((rows, D_MODEL), lambda i: (i, 0)),
                   pl.BlockSpec((SAMPLE_SEQS, FFN_K - 1, 2 * D_FF), lambda i: (i, 0, 0))),
        scratch_shapes=[pltpu.VMEM((SAMPLE_SEQS, FFN_PAD + SAMPLE_ROWS, 2 * D_FF), F32),
                        pltpu.VMEM((rows, D_FF), BF16)],
        compiler_params=_params(40),
        name="ffn_sample",
    )(up, h1, st, p["fconv_w"], p["fconv_b"], p["w_down"], p["g_final"])


def _head_expander():
    k = lax.broadcasted_iota(jnp.int32, (2 * LANES, D_SSM), 0) % LANES
    j = lax.broadcasted_iota(jnp.int32, (2 * LANES, D_SSM), 1) // HEAD_DIM
    return (k == j).astype(BF16)


def _lane_pad(v):
    return jnp.pad(v.astype(F32), (0, LANES - v.shape[0])).reshape(1, LANES)


def kernel(x_prompt, x_sample, state_conf_conv, state_xbc_conv, state_ssm, state_ffn_conv, meta_tokens, norm_mix_g, w_in, conf_conv_w, conf_conv_b, conf_ln_g, conf_ln_b, ssm_conv_w, ssm_conv_b, dt_bias, a_log, d_skip, ssm_norm_g, w_out, norm_ffn_g, w_up, ffn_conv_w, ffn_conv_b, w_down, norm_final_g):
    depth = w_in.shape[0]
    assert depth == 1, "one layer per step"
    nb, seq, _ = x_prompt.shape
    nsb, sseq, _ = x_sample.shape
    assert 2 * sseq == SAMPLE_ROWS and nsb % SAMPLE_SEQS == 0 and N_META <= SSD_CHUNK
    row = lambda v: v.astype(F32).reshape(1, -1)
    p = dict(
        conf_w=conf_conv_w[0], conf_b=row(conf_conv_b[0]), ln_g=row(conf_ln_g[0]),
        ln_b=row(conf_ln_b[0]), sconv_w=ssm_conv_w[0], sconv_b=row(ssm_conv_b[0]),
        dt_bias=_lane_pad(dt_bias[0]), a_log=_lane_pad(a_log[0]),
        d_exp=row(jnp.repeat(d_skip[0], HEAD_DIM)), snorm_g=row(ssm_norm_g[0]),
        e2=_head_expander(), fconv_w=ffn_conv_w[0], fconv_b=row(ffn_conv_b[0]),
        w_down=w_down[0].astype(BF16), g_final=row(norm_final_g))
    g_mix = row(norm_mix_g[0])
    g_ffn = row(norm_ffn_g[0])
    w_in_b = jnp.pad(w_in[0], ((0, 0), (0, PROJ_W - w_in.shape[2]))).astype(BF16)
    w_out_b = w_out[0].astype(BF16)
    w_up_b = w_up[0].astype(BF16)

    meta = jnp.concatenate([jnp.zeros((SSD_CHUNK - N_META, D_MODEL), F32),
                            meta_tokens.astype(F32)], axis=0)
    proj_m = _in_proj(meta, g_mix, w_in_b, SSD_CHUNK)
    cy_m, conf0, xbc0, ssm0 = _mix_p(
        proj_m, jnp.zeros((1, CONF_K - 1, D_CONV), F32), jnp.zeros((1, SSM_K - 1, XBC), F32),
        jnp.zeros((1, N_HEADS, HEAD_DIM, D_STATE), F32), p, 1, SSD_CHUNK, SSD_CHUNK - N_META)
    _, up_m = _out_up(cy_m, meta, w_out_b, g_ffn, w_up_b, SSD_CHUNK)
    ffn0 = up_m[SSD_CHUNK - FFN_PAD:, :]

    tile = 256
    xp = x_prompt.reshape(nb * seq, D_MODEL)
    proj_p = _in_proj(xp, g_mix, w_in_b, tile)
    cy_p, p_conf, p_xbc, p_ssm = _mix_p(proj_p, conf0, xbc0, ssm0, p, nb, tile, 0)
    h1_p, up_p = _out_up(cy_p, xp, w_out_b, g_ffn, w_up_b, tile)
    y_p, p_ffn = _ffn_p(up_p, h1_p, ffn0, p, nb, tile)

    xs = jnp.pad(x_sample, ((0, 0), (0, SAMPLE_ROWS - sseq), (0, 0))).reshape(
        nsb * SAMPLE_ROWS, D_MODEL)
    proj_s = _in_proj(xs, g_mix, w_in_b, tile)
    cy_s, s_conf, s_xbc, s_ssm = _mix_s(proj_s, state_conf_conv[0], state_xbc_conv[0],
                                        state_ssm[0], p)
    h1_s, up_s = _out_up(cy_s, xs, w_out_b, g_ffn, w_up_b, tile)
    y_s, s_ffn = _ffn_s(up_s, h1_s, state_ffn_conv[0], p)

    y_prompt = y_p.reshape(nb, seq, D_MODEL)
    y_sample = y_s.reshape(nsb, SAMPLE_ROWS, D_MODEL)[:, :sseq]
    return (y_prompt, y_sample, p_conf[None], p_xbc[None], p_ssm[None], p_ffn[None],
            s_conf[None], s_xbc[None], s_ssm[None], s_ffn[None])
```

```python
import functools

import jax
import jax.numpy as jnp
from jax import lax
from jax.experimental import pallas as pl
from jax.experimental.pallas import tpu as pltpu

F32 = jnp.float32
BF16 = jnp.bfloat16

D_MODEL = 1024
N_META = 16
D_CONV = 1024
D_SSM = 1024
HEAD_DIM = 64
N_HEADS = D_SSM // HEAD_DIM
N_GROUPS = 2
HEADS_PER_GROUP = N_HEADS // N_GROUPS
D_STATE = 128
GN = N_GROUPS * D_STATE
GROUP_W = D_SSM // N_GROUPS
XBC = D_SSM + 2 * GN
D_FF = 2816
CONF_K = 31
SSM_K = 4
FFN_K = 3
OFF_GATE = D_CONV
OFF_Z = 2 * D_CONV
OFF_XBC = OFF_Z + D_SSM
OFF_DT = OFF_XBC + XBC
LANES = 128
SUBLANES = 8
PROJ_W = OFF_DT + LANES
EPS = 1e-5

SSD_CHUNK = 128
SAMPLE_ROWS = 8
SAMPLE_SEQS = 8
MIB = 1024 * 1024


def _sigmoid(x):
    return 1.0 / (1.0 + jnp.exp(-x))


def _rms(x, g):
    return x * lax.rsqrt(jnp.mean(x * x, axis=-1, keepdims=True) + EPS) * g


def _mm(a, b):
    return jnp.dot(a, b, preferred_element_type=F32)


def _mm_nt(a, b):
    return lax.dot_general(a, b, (((1,), (1,)), ((), ())), preferred_element_type=F32)


def _const_spec(shape):
    zeros = (0,) * len(shape)
    return pl.BlockSpec(shape, lambda *_: zeros, pipeline_mode=pl.Buffered(1))


def _params(vmem_mib, ndims=1):
    return pltpu.CompilerParams(dimension_semantics=("arbitrary",) * ndims,
                                vmem_limit_bytes=vmem_mib * MIB)


def _in_proj_body(x_ref, g_ref, w_ref, o_ref):
    u = _rms(x_ref[...], g_ref[...])
    o_ref[...] = _mm(u.astype(BF16), w_ref[...])


def _in_proj(x, g, w, row_tile):
    rows = x.shape[0]
    return pl.pallas_call(
        _in_proj_body,
        out_shape=jax.ShapeDtypeStruct((rows, PROJ_W), F32),
        grid=(rows // row_tile,),
        in_specs=[pl.BlockSpec((row_tile, D_MODEL), lambda i: (i, 0)),
                  _const_spec((1, D_MODEL)),
                  _const_spec((D_MODEL, PROJ_W))],
        out_specs=pl.BlockSpec((row_tile, PROJ_W), lambda i: (i, 0)),
        compiler_params=_params(44),
        name="in_proj",
    )(x, g, w)


def _out_up_body(cy_ref, h_ref, wo_ref, g_ref, wu_ref, h1_ref, up_ref):
    h1 = h_ref[...] + _mm(cy_ref[...], wo_ref[...])
    h1_ref[...] = h1
    up_ref[...] = _mm(_rms(h1, g_ref[...]).astype(BF16), wu_ref[...])


def _out_up(cy, h, w_out, g, w_up, row_tile):
    rows = h.shape[0]
    return pl.pallas_call(
        _out_up_body,
        out_shape=(jax.ShapeDtypeStruct((rows, D_MODEL), F32),
                   jax.ShapeDtypeStruct((rows, 2 * D_FF), F32)),
        grid=(rows // row_tile,),
        in_specs=[pl.BlockSpec((row_tile, 2 * D_CONV), lambda i: (i, 0)),
                  pl.BlockSpec((row_tile, D_MODEL), lambda i: (i, 0)),
                  _const_spec((2 * D_CONV, D_MODEL)),
                  _const_spec((1, D_MODEL)),
                  _const_spec((D_MODEL, 2 * D_FF))],
        out_specs=(pl.BlockSpec((row_tile, D_MODEL), lambda i: (i, 0)),
                   pl.BlockSpec((row_tile, 2 * D_FF), lambda i: (i, 0))),
        compiler_params=_params(52),
        name="out_up",
    )(cy, h, w_out, g, w_up)


def _split3(v):
    v1 = v.astype(BF16)
    r = v - v1.astype(F32)
    v2 = r.astype(BF16)
    v3 = (r - v2.astype(F32)).astype(BF16)
    return v1, v2, v3


def _ssd_chunk(xs, bm, cm, dt, a, z, rows_per_seq, e2, d_exp, snorm_g, h_get, h_set):
    tile = xs.shape[0]
    nseq = tile // rows_per_seq
    shift = rows_per_seq.bit_length() - 1
    row = lax.broadcasted_iota(jnp.int32, (tile, tile), 0)
    col = lax.broadcasted_iota(jnp.int32, (tile, tile), 1)
    causal = col <= row
    if nseq > 1:
        same = lax.shift_right_logical(row, shift) == lax.shift_right_logical(col, shift)
        mask = jnp.logical_and(same, causal)
        bd = jnp.where(same, 1.0, 0.0).astype(BF16)
    else:
        mask = causal
        bd = jnp.ones((tile, tile), BF16)
    tri = jnp.where(mask, 1.0, 0.0).astype(BF16)

    a1, a2, a3 = _split3(a)
    acs = _mm(tri, a1) + _mm(tri, a2) + _mm(tri, a3)
    tot = _mm(bd, a1) + _mm(bd, a2) + _mm(bd, a3)
    acs_t = acs.T

    def expand(q):
        q1 = q.astype(BF16)
        q2 = (q - q1.astype(F32)).astype(BF16)
        return _mm(jnp.concatenate([q1, q2], axis=1), e2)

    xdt = xs * expand(dt)
    xw_t = (xs * expand(dt * jnp.exp(tot - acs))).T.astype(BF16)
    seg_e = expand(jnp.exp(acs))

    lane = lax.broadcasted_iota(jnp.int32, (tile, LANES), 1)
    low_half = lane < HEAD_DIM
    rowv = lax.broadcasted_iota(jnp.int32, (tile, D_STATE), 0)
    seq_of_row = lax.shift_right_logical(rowv, shift)

    ydiag = []
    yoff = []
    for g in range(N_GROUPS):
        cg = cm[:, g * D_STATE:(g + 1) * D_STATE]
        bg = bm[:, g * D_STATE:(g + 1) * D_STATE]
        cgb = cg.astype(BF16)
        bgb = bg.astype(BF16)
        cb = _mm_nt(cgb, bgb)
        for pair in range(HEADS_PER_GROUP // 2):
            blk = g * (HEADS_PER_GROUP // 2) + pair
            xb = xdt[:, blk * LANES:(blk + 1) * LANES]
            acc = None
            for half in range(2):
                h = 2 * blk + half
                diff = acs[:, h:h + 1] - acs_t[h:h + 1, :]
                decay = jnp.where(mask, jnp.exp(jnp.where(mask, diff, 0.0)), 0.0)
                xh = jnp.where(low_half if half == 0 else jnp.logical_not(low_half), xb, 0.0)
                term = _mm((cb * decay).astype(BF16), xh.astype(BF16))
                acc = term if acc is None else acc + term
            ydiag.append(acc)

        acc = None
        for b in range(nseq):
            hprev = h_get(b, g)
            if nseq > 1:
                sel = seq_of_row == b
                c_b = jnp.where(sel, cg, 0.0).astype(BF16)
                b_b = jnp.where(sel, bg, 0.0).astype(BF16)
            else:
                c_b, b_b = cgb, bgb
            term = _mm_nt(c_b, hprev.astype(BF16))
            acc = term if acc is None else acc + term
            dec = jnp.exp(tot[b * rows_per_seq:b * rows_per_seq + 1, :])
            scaled = [hprev[e * HEAD_DIM:(e + 1) * HEAD_DIM, :]
                      * dec[:, g * HEADS_PER_GROUP + e:g * HEADS_PER_GROUP + e + 1]
                      for e in range(HEADS_PER_GROUP)]
            upd = _mm(xw_t[g * GROUP_W:(g + 1) * GROUP_W, :], b_b)
            h_set(b, g, jnp.concatenate(scaled, axis=0) + upd)
        yoff.append(acc)

    y = (jnp.concatenate(ydiag, axis=1) + jnp.concatenate(yoff, axis=1) * seg_e + xs * d_exp)
    yz = y * (z * _sigmoid(z))
    outs = []
    for g in range(N_GROUPS):
        blk = yz[:, g * GROUP_W:(g + 1) * GROUP_W]
        ms = jnp.mean(blk * blk, axis=-1, keepdims=True)
        outs.append(blk * lax.rsqrt(ms + EPS) * snorm_g[:, g * GROUP_W:(g + 1) * GROUP_W])
    return jnp.concatenate(outs, axis=1)


def _dt_and_a(dt_raw, dtb, alog, pad_rows, valid_rows_per_tile=None):
    v = dt_raw + dtb
    dt = jnp.maximum(v, 0.0) + jnp.log1p(jnp.exp(-jnp.abs(v)))
    lane = lax.broadcasted_iota(jnp.int32, dt.shape, 1)
    keep = lane < N_HEADS
    row = lax.broadcasted_iota(jnp.int32, dt.shape, 0)
    if pad_rows:
        keep = jnp.logical_and(row >= pad_rows, keep)
    if valid_rows_per_tile is not None:
        keep = jnp.logical_and((row & (SAMPLE_ROWS - 1)) < valid_rows_per_tile, keep)
    dt = jnp.where(keep, dt, 0.0)
    return dt, dt * (-jnp.exp(alog))


def _layernorm_silu(cv, g, b):
    mu = jnp.mean(cv, axis=-1, keepdims=True)
    d = cv - mu
    var = jnp.mean(d * d, axis=-1, keepdims=True)
    cn = d * lax.rsqrt(var + EPS) * g + b
    return cn * _sigmoid(cn)


CONV_ROWS = 32
XCONV_ROWS = 64
CONF_PAD = 32
XBC_PAD = 8
PROJ_COLS = 256


def _lane_blocks(width):
    return width // LANES


def _mixer_p_body(pad_rows, x_ref, gm_ref, win_ref, conf0_ref, xbc0_ref, ssm0_ref, cw_ref,
                  cb_ref, lng_ref, lnb_ref, sw_ref, sb_ref, dtb_ref, alog_ref, dexp_ref, sng_ref,
                  e2_ref, wout_ref, h1_ref, conf_out_ref, xbc_out_ref, ssm_out_ref,
                  gbuf, xbuf, hst, cbuf, zbuf, xact, cy):
    c = pl.program_id(1)
    last = pl.num_programs(1) - 1
    tile = x_ref.shape[0]
    conf_lo = CONF_PAD - (CONF_K - 1)
    xbc_lo = XBC_PAD - (SSM_K - 1)

    @pl.when(c == 0)
    def _():
        for j in range(_lane_blocks(D_CONV)):
            gbuf[j, conf_lo:CONF_PAD, :] = conf0_ref[0, :, j * LANES:(j + 1) * LANES]
        for j in range(_lane_blocks(XBC)):
            xbuf[j, xbc_lo:XBC_PAD, :] = xbc0_ref[0, :, j * LANES:(j + 1) * LANES]
        hst[...] = ssm0_ref[0].reshape(N_HEADS * HEAD_DIM, D_STATE)

    x = x_ref[...]
    u = _rms(x, gm_ref[...]).astype(BF16)

    per = PROJ_COLS // LANES
    for i in range(D_CONV // PROJ_COLS):
        lin = _mm(u, win_ref[:, i * PROJ_COLS:(i + 1) * PROJ_COLS])
        gate = _mm(u, win_ref[:, OFF_GATE + i * PROJ_COLS:OFF_GATE + (i + 1) * PROJ_COLS])
        glu = lin * _sigmoid(gate)
        for jj in range(per):
            gbuf[i * per + jj, CONF_PAD:CONF_PAD + tile, :] = glu[:, jj * LANES:(jj + 1) * LANES]
    for i in range(D_SSM // PROJ_COLS):
        zbuf[:, i * PROJ_COLS:(i + 1) * PROJ_COLS] = _mm(
            u, win_ref[:, OFF_Z + i * PROJ_COLS:OFF_Z + (i + 1) * PROJ_COLS])
    for i in range(XBC // PROJ_COLS):
        res = _mm(u, win_ref[:, OFF_XBC + i * PROJ_COLS:OFF_XBC + (i + 1) * PROJ_COLS])
        for jj in range(per):
            xbuf[i * per + jj, XBC_PAD:XBC_PAD + tile, :] = res[:, jj * LANES:(jj + 1) * LANES]
    dt_raw = _mm(u, win_ref[:, OFF_DT:PROJ_W])

    for rb in range(tile // CONV_ROWS):
        for j in range(_lane_blocks(D_CONV)):
            cols = slice(j * LANES, (j + 1) * LANES)
            acc = jnp.broadcast_to(cb_ref[:, cols], (CONV_ROWS, LANES))
            for k in range(CONF_K):
                r0 = rb * CONV_ROWS + conf_lo + k
                acc = acc + cw_ref[k:k + 1, cols] * gbuf[j, r0:r0 + CONV_ROWS, :]
            cbuf[rb * CONV_ROWS:(rb + 1) * CONV_ROWS, cols] = acc
    cy[:, 0:D_CONV] = _layernorm_silu(cbuf[...], lng_ref[...], lnb_ref[...]).astype(BF16)

    @pl.when(c == last)
    def _():
        for j in range(_lane_blocks(D_CONV)):
            conf_out_ref[0, :, j * LANES:(j + 1) * LANES] = gbuf[j, tile + conf_lo:tile + CONF_PAD, :]

    gbuf[:, 0:CONF_PAD, :] = gbuf[:, tile:tile + CONF_PAD, :]

    for rb in range(tile // XCONV_ROWS):
        for j in range(_lane_blocks(XBC)):
            cols = slice(j * LANES, (j + 1) * LANES)
            acc = jnp.broadcast_to(sb_ref[:, cols], (XCONV_ROWS, LANES))
            for k in range(SSM_K):
                r0 = rb * XCONV_ROWS + xbc_lo + k
                acc = acc + sw_ref[k:k + 1, cols] * xbuf[j, r0:r0 + XCONV_ROWS, :]
            xact[rb * XCONV_ROWS:(rb + 1) * XCONV_ROWS, cols] = acc * _sigmoid(acc)

    @pl.when(c == last)
    def _():
        for j in range(_lane_blocks(XBC)):
            xbc_out_ref[0, :, j * LANES:(j + 1) * LANES] = xbuf[j, tile + xbc_lo:tile + XBC_PAD, :]

    xbuf[:, 0:XBC_PAD, :] = xbuf[:, tile:tile + XBC_PAD, :]

    dt, a = _dt_and_a(dt_raw, dtb_ref[...], alog_ref[...], pad_rows)

    def h_get(b, g):
        return hst[g * GROUP_W:(g + 1) * GROUP_W, :]

    def h_set(b, g, val):
        hst[g * GROUP_W:(g + 1) * GROUP_W, :] = val

    for j in range(tile // SSD_CHUNK):
        rs = slice(j * SSD_CHUNK, (j + 1) * SSD_CHUNK)
        out = _ssd_chunk(xact[rs, 0:D_SSM], xact[rs, D_SSM:D_SSM + GN], xact[rs, D_SSM + GN:XBC],
                         dt[rs], a[rs], zbuf[rs, :], SSD_CHUNK, e2_ref[...],
                         dexp_ref[...], sng_ref[...], h_get, h_set)
        cy[rs, D_CONV:2 * D_CONV] = out.astype(BF16)

    @pl.when(c == last)
    def _():
        ssm_out_ref[0] = hst[...].reshape(N_HEADS, HEAD_DIM, D_STATE)

    h1_ref[...] = x + _mm(cy[...], wout_ref[...])


def _mixer_p(x, conf0, xbc0, ssm0, p, nbatch, tile, pad_rows):
    rows = x.shape[0]
    nchunk = rows // (nbatch * tile)
    row_map = lambda b, c: (b * nchunk + c, 0)
    state_specs = lambda shp: pl.BlockSpec((1,) + shp, lambda b, c: (b,) + (0,) * len(shp))
    small = [p["conf_w"], p["conf_b"], p["ln_g"], p["ln_b"], p["sconv_w"], p["sconv_b"],
             p["dt_bias"], p["a_log"], p["d_exp"], p["snorm_g"], p["e2"]]
    return pl.pallas_call(
        functools.partial(_mixer_p_body, pad_rows),
        out_shape=(jax.ShapeDtypeStruct((rows, D_MODEL), F32),
                   jax.ShapeDtypeStruct((nbatch, CONF_K - 1, D_CONV), F32),
                   jax.ShapeDtypeStruct((nbatch, SSM_K - 1, XBC), F32),
                   jax.ShapeDtypeStruct((nbatch, N_HEADS, HEAD_DIM, D_STATE), F32)),
        grid=(nbatch, nchunk),
        in_specs=[pl.BlockSpec((tile, D_MODEL), row_map),
                  _const_spec((1, D_MODEL)),
                  _const_spec((D_MODEL, PROJ_W)),
                  _const_spec((1, CONF_K - 1, D_CONV)),
                  _const_spec((1, SSM_K - 1, XBC)),
                  _const_spec((1, N_HEADS, HEAD_DIM, D_STATE))]
                 + [_const_spec(s.shape) for s in small]
                 + [_const_spec((2 * D_CONV, D_MODEL))],
        out_specs=(pl.BlockSpec((tile, D_MODEL), row_map),
                   state_specs((CONF_K - 1, D_CONV)),
                   state_specs((SSM_K - 1, XBC)),
                   state_specs((N_HEADS, HEAD_DIM, D_STATE))),
        scratch_shapes=[pltpu.VMEM((_lane_blocks(D_CONV), tile + CONF_PAD, LANES), F32),
                        pltpu.VMEM((_lane_blocks(XBC), tile + XBC_PAD, LANES), F32),
                        pltpu.VMEM((N_HEADS * HEAD_DIM, D_STATE), F32),
                        pltpu.VMEM((tile, D_CONV), F32),
                        pltpu.VMEM((tile, D_SSM), F32),
                        pltpu.VMEM((tile, XBC), F32),
                        pltpu.VMEM((tile, 2 * D_CONV), BF16)],
        compiler_params=_params(48, 2),
        name="mixer_prompt",
    )(x, p["g_mix"], p["w_in"], conf0, xbc0, ssm0, *small, p["w_out"])


def _mix_s_body(proj_ref, conf_ref, xbc_ref, ssm_ref, cw_ref, cb_ref, lng_ref, lnb_ref, sw_ref,
                sb_ref, dtb_ref, alog_ref, dexp_ref, sng_ref, e2_ref,
                cy_ref, conf_out_ref, xbc_out_ref, ssm_out_ref, fbuf, xfb):
    nseq = conf_ref.shape[0]
    rows = nseq * SAMPLE_ROWS
    new_rows = proj_ref.shape[0] // nseq // 2
    conf_lo = CONF_PAD - (CONF_K - 1)
    xbc_lo = XBC_PAD - (SSM_K - 1)

    glu = proj_ref[:, 0:D_CONV] * _sigmoid(proj_ref[:, OFF_GATE:OFF_Z])
    fbuf[:, conf_lo:CONF_PAD, :] = conf_ref[...]
    fbuf[:, CONF_PAD:CONF_PAD + SAMPLE_ROWS, :] = glu.reshape(nseq, SAMPLE_ROWS, D_CONV)
    acc = jnp.broadcast_to(cb_ref[...].reshape(1, 1, D_CONV), (nseq, SAMPLE_ROWS, D_CONV))
    for k in range(CONF_K):
        w = cw_ref[k:k + 1, :].reshape(1, 1, D_CONV)
        acc = acc + w * fbuf[:, conf_lo + k:conf_lo + k + SAMPLE_ROWS, :]
    conf_out_ref[...] = fbuf[:, conf_lo + new_rows:CONF_PAD + new_rows, :]
    cy_ref[:, 0:D_CONV] = _layernorm_silu(acc.reshape(rows, D_CONV), lng_ref[...],
                                          lnb_ref[...]).astype(BF16)

    xfb[:, xbc_lo:XBC_PAD, :] = xbc_ref[...]
    xfb[:, XBC_PAD:XBC_PAD + SAMPLE_ROWS, :] = proj_ref[:, OFF_XBC:OFF_DT].reshape(
        nseq, SAMPLE_ROWS, XBC)
    xv = jnp.broadcast_to(sb_ref[...].reshape(1, 1, XBC), (nseq, SAMPLE_ROWS, XBC))
    for k in range(SSM_K):
        w = sw_ref[k:k + 1, :].reshape(1, 1, XBC)
        xv = xv + w * xfb[:, xbc_lo + k:xbc_lo + k + SAMPLE_ROWS, :]
    xbc_out_ref[...] = xfb[:, xbc_lo + new_rows:XBC_PAD + new_rows, :]
    xv = xv.reshape(rows, XBC)
    xv = xv * _sigmoid(xv)

    dt, a = _dt_and_a(proj_ref[:, OFF_DT:PROJ_W], dtb_ref[...], alog_ref[...], 0, new_rows)

    def h_get(b, g):
        return ssm_ref[b, g * HEADS_PER_GROUP:(g + 1) * HEADS_PER_GROUP].reshape(GROUP_W, D_STATE)

    def h_set(b, g, val):
        ssm_out_ref[b, g * HEADS_PER_GROUP:(g + 1) * HEADS_PER_GROUP] = val.reshape(
            HEADS_PER_GROUP, HEAD_DIM, D_STATE)

    out = _ssd_chunk(xv[:, 0:D_SSM], xv[:, D_SSM:D_SSM + GN], xv[:, D_SSM + GN:XBC], dt, a,
                     proj_ref[:, OFF_Z:OFF_XBC], SAMPLE_ROWS, e2_ref[...], dexp_ref[...],
                     sng_ref[...], h_get, h_set)
    cy_ref[:, D_CONV:2 * D_CONV] = out.astype(BF16)


def _mix_s(proj, conf, xbc, ssm, p):
    nbatch = conf.shape[0]
    rows = SAMPLE_SEQS * SAMPLE_ROWS
    small = [p["conf_w"], p["conf_b"], p["ln_g"], p["ln_b"], p["sconv_w"], p["sconv_b"],
             p["dt_bias"], p["a_log"], p["d_exp"], p["snorm_g"], p["e2"]]
    blk3 = lambda shp: pl.BlockSpec((SAMPLE_SEQS,) + shp, lambda i: (i,) + (0,) * len(shp))
    return pl.pallas_call(
        _mix_s_body,
        out_shape=(jax.ShapeDtypeStruct((nbatch * SAMPLE_ROWS, 2 * D_CONV), BF16),
                   jax.ShapeDtypeStruct(conf.shape, F32),
                   jax.ShapeDtypeStruct(xbc.shape, F32),
                   jax.ShapeDtypeStruct(ssm.shape, F32)),
        grid=(nbatch // SAMPLE_SEQS,),
        in_specs=[pl.BlockSpec((rows, PROJ_W), lambda i: (i, 0)),
                  blk3((CONF_K - 1, D_CONV)), blk3((SSM_K - 1, XBC)),
                  blk3((N_HEADS, HEAD_DIM, D_STATE))]
                 + [_const_spec(s.shape) for s in small],
        out_specs=(pl.BlockSpec((rows, 2 * D_CONV), lambda i: (i, 0)),
                   blk3((CONF_K - 1, D_CONV)), blk3((SSM_K - 1, XBC)),
                   blk3((N_HEADS, HEAD_DIM, D_STATE))),
        scratch_shapes=[pltpu.VMEM((SAMPLE_SEQS, CONF_PAD + SAMPLE_ROWS, D_CONV), F32),
                        pltpu.VMEM((SAMPLE_SEQS, XBC_PAD + SAMPLE_ROWS, XBC), F32)],
        compiler_params=_params(40),
        name="mix_sample",
    )(proj, conf, xbc, ssm, *small)


FFN_COLS = 256
FFN_PAD = 8


def _ffn_p_body(h1_ref, gn_ref, wu_ref, tail0_ref, cw_ref, cb_ref, wd_ref, gf_ref,
                y_ref, tail_out_ref, ubuf, act):
    c = pl.program_id(1)
    tile = h1_ref.shape[0]
    lo = FFN_PAD - (FFN_K - 1)
    nblk = _lane_blocks(2 * D_FF)
    half = _lane_blocks(D_FF)

    @pl.when(c == 0)
    def _():
        for j in range(nblk):
            ubuf[j, lo:FFN_PAD, :] = tail0_ref[0, :, j * LANES:(j + 1) * LANES]

    h1 = h1_ref[...]
    u = _rms(h1, gn_ref[...]).astype(BF16)
    per = PROJ_COLS // LANES
    for i in range(2 * D_FF // PROJ_COLS):
        res = _mm(u, wu_ref[:, i * PROJ_COLS:(i + 1) * PROJ_COLS])
        for jj in range(per):
            ubuf[i * per + jj, FFN_PAD:FFN_PAD + tile, :] = res[:, jj * LANES:(jj + 1) * LANES]

    def conv(j, r0):
        cols = slice(j * LANES, (j + 1) * LANES)
        out = jnp.broadcast_to(cb_ref[:, cols], (XCONV_ROWS, LANES))
        for k in range(FFN_K):
            out = out + cw_ref[k:k + 1, cols] * ubuf[j, r0 + lo + k:r0 + lo + k + XCONV_ROWS, :]
        return out

    for rb in range(tile // XCONV_ROWS):
        r0 = rb * XCONV_ROWS
        for j in range(half):
            gate = conv(j, r0)
            lin = conv(half + j, r0)
            act[r0:r0 + XCONV_ROWS, j * LANES:(j + 1) * LANES] = (
                gate * _sigmoid(gate) * lin).astype(BF16)

    @pl.when(c == pl.num_programs(1) - 1)
    def _():
        for j in range(nblk):
            tail_out_ref[0, :, j * LANES:(j + 1) * LANES] = ubuf[j, tile + lo:tile + FFN_PAD, :]

    ubuf[:, 0:FFN_PAD, :] = ubuf[:, tile:tile + FFN_PAD, :]

    y_ref[...] = _rms(h1 + _mm(act[...], wd_ref[...]), gf_ref[...])


def _ffn_p(h1, tail0, p, nbatch, tile):
    rows = h1.shape[0]
    nchunk = rows // (nbatch * tile)
    row_map = lambda b, c: (b * nchunk + c, 0)
    return pl.pallas_call(
        _ffn_p_body,
        out_shape=(jax.ShapeDtypeStruct((rows, D_MODEL), F32),
                   jax.ShapeDtypeStruct((nbatch, FFN_K - 1, 2 * D_FF), F32)),
        grid=(nbatch, nchunk),
        in_specs=[pl.BlockSpec((tile, D_MODEL), row_map),
                  _const_spec((1, D_MODEL)),
                  _const_spec((D_MODEL, 2 * D_FF)),
                  _const_spec((1, FFN_K - 1, 2 * D_FF)),
                  _const_spec((FFN_K, 2 * D_FF)),
                  _const_spec((1, 2 * D_FF)),
                  _const_spec((D_FF, D_MODEL)),
                  _const_spec((1, D_MODEL))],
        out_specs=(pl.BlockSpec((tile, D_MODEL), row_map),
                   pl.BlockSpec((1, FFN_K - 1, 2 * D_FF), lambda b, c: (b, 0, 0))),
        scratch_shapes=[pltpu.VMEM((_lane_blocks(2 * D_FF), tile + FFN_PAD, LANES), F32),
                        pltpu.VMEM((tile, D_FF), BF16)],
        compiler_params=_params(48, 2),
        name="ffn_prompt",
    )(h1, p["g_ffn"], p["w_up"], tail0, p["fconv_w"], p["fconv_b"], p["w_down"], p["g_final"])


def _ffn_s_body(up_ref, h1_ref, st_ref, cw_ref, cb_ref, wd_ref, gf_ref,
                y_ref, st_out_ref, ubuf, act):
    nseq = st_ref.shape[0]
    rows = nseq * SAMPLE_ROWS
    new_rows = SAMPLE_ROWS // 2
    lo = FFN_PAD - (FFN_K - 1)
    ubuf[:, lo:FFN_PAD, :] = st_ref[...]
    ubuf[:, FFN_PAD:FFN_PAD + SAMPLE_ROWS, :] = up_ref[...].reshape(nseq, SAMPLE_ROWS, 2 * D_FF)

    def conv(cols):
        out = jnp.broadcast_to(cb_ref[:, cols].reshape(1, 1, FFN_COLS),
                               (nseq, SAMPLE_ROWS, FFN_COLS))
        for k in range(FFN_K):
            w = cw_ref[k:k + 1, cols].reshape(1, 1, FFN_COLS)
            out = out + w * ubuf[:, lo + k:lo + k + SAMPLE_ROWS, cols]
        return out.reshape(rows, FFN_COLS)

    for j in range(D_FF // FFN_COLS):
        gate = conv(slice(j * FFN_COLS, (j + 1) * FFN_COLS))
        lin = conv(slice(D_FF + j * FFN_COLS, D_FF + (j + 1) * FFN_COLS))
        act[:, j * FFN_COLS:(j + 1) * FFN_COLS] = (gate * _sigmoid(gate) * lin).astype(BF16)

    st_out_ref[...] = ubuf[:, lo + new_rows:FFN_PAD + new_rows, :]
    y_ref[...] = _rms(h1_ref[...] + _mm(act[...], wd_ref[...]), gf_ref[...])


def _ffn_s(up, h1, st, p):
    nbatch = st.shape[0]
    rows = SAMPLE_SEQS * SAMPLE_ROWS
    return pl.pallas_call(
        _ffn_s_body,
        out_shape=(jax.ShapeDtypeStruct((nbatch * SAMPLE_ROWS, D_MODEL), F32),
                   jax.ShapeDtypeStruct(st.shape, F32)),
        grid=(nbatch // SAMPLE_SEQS,),
        in_specs=[pl.BlockSpec((rows, 2 * D_FF), lambda i: (i, 0)),
                  pl.BlockSpec((rows, D_MODEL), lambda i: (i, 0)),
                  pl.BlockSpec((SAMPLE_SEQS, FFN_K - 1, 2 * D_FF), lambda i: (i, 0, 0)),
                  _const_spec((FFN_K, 2 * D_FF)),
                  _const_spec((1, 2 * D_FF)),
                  _const_spec((D_FF, D_MODEL)),
                  _const_spec((1, D_MODEL))],
        out_specs=(pl.BlockSpec((rows, D_MODEL), lambda i: (i, 0)),
                   pl.BlockSpec((SAMPLE_SEQS, FFN_K - 1, 2 * D_FF), lambda i: (i, 0, 0))),
        scratch_shapes=[pltpu.VMEM((SAMPLE_SEQS, FFN_PAD + SAMPLE_ROWS, 2 * D_FF), F32),
                        pltpu.VMEM((rows, D_FF), BF16)],
        compiler_params=_params(40),
        name="ffn_sample",
    )(up, h1, st, p["fconv_w"], p["fconv_b"], p["w_down"], p["g_final"])


def _head_expander():
    k = lax.broadcasted_iota(jnp.int32, (2 * LANES, D_SSM), 0) % LANES
    j = lax.broadcasted_iota(jnp.int32, (2 * LANES, D_SSM), 1) // HEAD_DIM
    return (k == j).astype(BF16)


def _lane_pad(v):
    return jnp.pad(v.astype(F32), (0, LANES - v.shape[0])).reshape(1, LANES)


def kernel(x_prompt, x_sample, state_conf_conv, state_xbc_conv, state_ssm, state_ffn_conv, meta_tokens, norm_mix_g, w_in, conf_conv_w, conf_conv_b, conf_ln_g, conf_ln_b, ssm_conv_w, ssm_conv_b, dt_bias, a_log, d_skip, ssm_norm_g, w_out, norm_ffn_g, w_up, ffn_conv_w, ffn_conv_b, w_down, norm_final_g):
    depth = w_in.shape[0]
    assert depth == 1, "one layer per step"
    nb, seq, _ = x_prompt.shape
    nsb, sseq, _ = x_sample.shape
    assert 2 * sseq == SAMPLE_ROWS and nsb % SAMPLE_SEQS == 0 and N_META <= SSD_CHUNK
    row = lambda v: v.astype(F32).reshape(1, -1)
    p = dict(
        conf_w=conf_conv_w[0], conf_b=row(conf_conv_b[0]), ln_g=row(conf_ln_g[0]),
        ln_b=row(conf_ln_b[0]), sconv_w=ssm_conv_w[0], sconv_b=row(ssm_conv_b[0]),
        dt_bias=_lane_pad(dt_bias[0]), a_log=_lane_pad(a_log[0]),
        d_exp=row(jnp.repeat(d_skip[0], HEAD_DIM)), snorm_g=row(ssm_norm_g[0]),
        e2=_head_expander(), fconv_w=ffn_conv_w[0], fconv_b=row(ffn_conv_b[0]),
        w_down=w_down[0].astype(BF16), g_final=row(norm_final_g),
        g_mix=row(norm_mix_g[0]), g_ffn=row(norm_ffn_g[0]),
        w_in=jnp.pad(w_in[0], ((0, 0), (0, PROJ_W - w_in.shape[2]))).astype(BF16),
        w_out=w_out[0].astype(BF16), w_up=w_up[0].astype(BF16))
    g_mix, g_ffn, w_in_b, w_out_b, w_up_b = p["g_mix"], p["g_ffn"], p["w_in"], p["w_out"], p["w_up"]

    meta = jnp.concatenate([jnp.zeros((SSD_CHUNK - N_META, D_MODEL), F32),
                            meta_tokens.astype(F32)], axis=0)
    h1_m, conf0, xbc0, ssm0 = _mixer_p(
        meta, jnp.zeros((1, CONF_K - 1, D_CONV), F32), jnp.zeros((1, SSM_K - 1, XBC), F32),
        jnp.zeros((1, N_HEADS, HEAD_DIM, D_STATE), F32), p, 1, SSD_CHUNK, SSD_CHUNK - N_META)
    _, ffn0 = _ffn_p(h1_m, jnp.zeros((1, FFN_K - 1, 2 * D_FF), F32), p, 1, SSD_CHUNK)

    tile = 256
    xp = x_prompt.reshape(nb * seq, D_MODEL)
    h1_p, p_conf, p_xbc, p_ssm = _mixer_p(xp, conf0, xbc0, ssm0, p, nb, tile, 0)
    y_p, p_ffn = _ffn_p(h1_p, ffn0, p, nb, tile)

    xs = jnp.pad(x_sample, ((0, 0), (0, SAMPLE_ROWS - sseq), (0, 0))).reshape(
        nsb * SAMPLE_ROWS, D_MODEL)
    proj_s = _in_proj(xs, g_mix, w_in_b, tile)
    cy_s, s_conf, s_xbc, s_ssm = _mix_s(proj_s, state_conf_conv[0], state_xbc_conv[0],
                                        state_ssm[0], p)
    h1_s, up_s = _out_up(cy_s, xs, w_out_b, g_ffn, w_up_b, tile)
    y_s, s_ffn = _ffn_s(up_s, h1_s, state_ffn_conv[0], p)

    y_prompt = y_p.reshape(nb, seq, D_MODEL)
    y_sample = y_s.reshape(nsb, SAMPLE_ROWS, D_MODEL)[:, :sseq]
    return (y_prompt, y_sample, p_conf[None], p_xbc[None], p_ssm[None], p_ffn[None],
            s_conf[None], s_xbc[None], s_ssm[None], s_ffn[None])
```

```python
import functools

import jax
import jax.numpy as jnp
from jax import lax
from jax.experimental import pallas as pl
from jax.experimental.pallas import tpu as pltpu

F32 = jnp.float32
BF16 = jnp.bfloat16

D_MODEL = 1024
N_META = 16
D_CONV = 1024
D_SSM = 1024
HEAD_DIM = 64
N_HEADS = D_SSM // HEAD_DIM
N_GROUPS = 2
HEADS_PER_GROUP = N_HEADS // N_GROUPS
D_STATE = 128
GN = N_GROUPS * D_STATE
GROUP_W = D_SSM // N_GROUPS
XBC = D_SSM + 2 * GN
D_FF = 2816
CONF_K = 31
SSM_K = 4
FFN_K = 3
OFF_GATE = D_CONV
OFF_Z = 2 * D_CONV
OFF_XBC = OFF_Z + D_SSM
OFF_DT = OFF_XBC + XBC
LANES = 128
SUBLANES = 8
PROJ_W = OFF_DT + LANES
EPS = 1e-5

SSD_CHUNK = 128
SAMPLE_ROWS = 8
SAMPLE_SEQS = 8
MIB = 1024 * 1024


def _sigmoid(x):
    return 1.0 / (1.0 + jnp.exp(-x))


def _rms(x, g):
    return x * lax.rsqrt(jnp.mean(x * x, axis=-1, keepdims=True) + EPS) * g


def _mm(a, b):
    return jnp.dot(a, b, preferred_element_type=F32)


def _mm_nt(a, b):
    return lax.dot_general(a, b, (((1,), (1,)), ((), ())), preferred_element_type=F32)


def _const_spec(shape):
    zeros = (0,) * len(shape)
    return pl.BlockSpec(shape, lambda *_: zeros, pipeline_mode=pl.Buffered(1))


def _params(vmem_mib, ndims=1, flags=None):
    return pltpu.CompilerParams(dimension_semantics=("arbitrary",) * ndims,
                                vmem_limit_bytes=vmem_mib * MIB, flags=flags)


def _in_proj_body(x_ref, g_ref, w_ref, wdt_ref, o_ref):
    u = _rms(x_ref[...], g_ref[...]).astype(BF16)
    o_ref[:, 0:OFF_DT] = _mm(u, w_ref[...])
    o_ref[:, OFF_DT:PROJ_W] = _mm(u, wdt_ref[...])


def _in_proj(x, g, w, w_dt, row_tile):
    rows = x.shape[0]
    return pl.pallas_call(
        _in_proj_body,
        out_shape=jax.ShapeDtypeStruct((rows, PROJ_W), F32),
        grid=(rows // row_tile,),
        in_specs=[pl.BlockSpec((row_tile, D_MODEL), lambda i: (i, 0)),
                  _const_spec((1, D_MODEL)),
                  _const_spec((D_MODEL, OFF_DT)),
                  _const_spec((D_MODEL, LANES))],
        out_specs=pl.BlockSpec((row_tile, PROJ_W), lambda i: (i, 0)),
        compiler_params=_params(44),
        name="in_proj",
    )(x, g, w, w_dt)


def _out_up_body(cy_ref, h_ref, wo_ref, g_ref, wu_ref, h1_ref, up_ref):
    h1 = h_ref[...] + _mm(cy_ref[...], wo_ref[...])
    h1_ref[...] = h1
    up_ref[...] = _mm(_rms(h1, g_ref[...]).astype(BF16), wu_ref[...])


def _out_up(cy, h, w_out, g, w_up, row_tile):
    rows = h.shape[0]
    return pl.pallas_call(
        _out_up_body,
        out_shape=(jax.ShapeDtypeStruct((rows, D_MODEL), F32),
                   jax.ShapeDtypeStruct((rows, 2 * D_FF), F32)),
        grid=(rows // row_tile,),
        in_specs=[pl.BlockSpec((row_tile, 2 * D_CONV), lambda i: (i, 0)),
                  pl.BlockSpec((row_tile, D_MODEL), lambda i: (i, 0)),
                  _const_spec((2 * D_CONV, D_MODEL)),
                  _const_spec((1, D_MODEL)),
                  _const_spec((D_MODEL, 2 * D_FF))],
        out_specs=(pl.BlockSpec((row_tile, D_MODEL), lambda i: (i, 0)),
                   pl.BlockSpec((row_tile, 2 * D_FF), lambda i: (i, 0))),
        compiler_params=_params(52),
        name="out_up",
    )(cy, h, w_out, g, w_up)


def _split3(v):
    v1 = v.astype(BF16)
    r = v - v1.astype(F32)
    v2 = r.astype(BF16)
    v3 = (r - v2.astype(F32)).astype(BF16)
    return v1, v2, v3


def _ssd_intra(xs, bm, cm, dt, a, rows_per_seq, e2, d_exp, tick=lambda: None):
    tile = xs.shape[0]
    nseq = tile // rows_per_seq
    shift = rows_per_seq.bit_length() - 1
    row = lax.broadcasted_iota(jnp.int32, (tile, tile), 0)
    col = lax.broadcasted_iota(jnp.int32, (tile, tile), 1)
    causal = col <= row
    if nseq > 1:
        same = lax.shift_right_logical(row, shift) == lax.shift_right_logical(col, shift)
        mask = jnp.logical_and(same, causal)
        bd = jnp.where(same, 1.0, 0.0).astype(BF16)
    else:
        mask = causal
        bd = jnp.ones((tile, tile), BF16)
    tri = jnp.where(mask, 1.0, 0.0).astype(BF16)

    def mm(p, q, nt=False):
        out = _mm_nt(p, q) if nt else _mm(p, q)
        tick()
        return out

    a1, a2, a3 = _split3(a)
    acs = mm(tri, a1) + mm(tri, a2) + mm(tri, a3)
    tot = mm(bd, a1) + mm(bd, a2) + mm(bd, a3)
    acs_t = acs.T

    def expand(q):
        q1 = q.astype(BF16)
        q2 = (q - q1.astype(F32)).astype(BF16)
        return mm(jnp.concatenate([q1, q2], axis=1), e2)

    xdt = xs * expand(dt)
    xw_t = (xs * expand(dt * jnp.exp(tot - acs))).T.astype(BF16)
    seg_e = expand(jnp.exp(acs))

    lane = lax.broadcasted_iota(jnp.int32, (tile, LANES), 1)
    low_half = lane < HEAD_DIM
    ydiag = []
    for g in range(N_GROUPS):
        cgb = cm[:, g * D_STATE:(g + 1) * D_STATE].astype(BF16)
        bgb = bm[:, g * D_STATE:(g + 1) * D_STATE].astype(BF16)
        cb = mm(cgb, bgb, nt=True)
        for pair in range(HEADS_PER_GROUP // 2):
            blk = g * (HEADS_PER_GROUP // 2) + pair
            xb = xdt[:, blk * LANES:(blk + 1) * LANES]
            acc = None
            for half in range(2):
                h = 2 * blk + half
                diff = acs[:, h:h + 1] - acs_t[h:h + 1, :]
                decay = jnp.where(mask, jnp.exp(jnp.where(mask, diff, 0.0)), 0.0)
                xh = jnp.where(low_half if half == 0 else jnp.logical_not(low_half), xb, 0.0)
                term = mm((cb * decay).astype(BF16), xh.astype(BF16))
                acc = term if acc is None else acc + term
            ydiag.append(acc)
    return dict(base=jnp.concatenate(ydiag, axis=1) + xs * d_exp, xw_t=xw_t, seg_e=seg_e,
                tot=tot, bm=bm, cm=cm)


def _ssd_inter(part, z, rows_per_seq, snorm_g, h_get, h_set, tick=lambda: None):
    bm, cm, tot, xw_t = part["bm"], part["cm"], part["tot"], part["xw_t"]
    tile = bm.shape[0]
    nseq = tile // rows_per_seq
    shift = rows_per_seq.bit_length() - 1
    rowv = lax.broadcasted_iota(jnp.int32, (tile, D_STATE), 0)
    seq_of_row = lax.shift_right_logical(rowv, shift)
    yoff = []
    for g in range(N_GROUPS):
        cg = cm[:, g * D_STATE:(g + 1) * D_STATE]
        bg = bm[:, g * D_STATE:(g + 1) * D_STATE]
        acc = None
        for b in range(nseq):
            hprev = h_get(b, g)
            if nseq > 1:
                sel = seq_of_row == b
                c_b = jnp.where(sel, cg, 0.0).astype(BF16)
                b_b = jnp.where(sel, bg, 0.0).astype(BF16)
            else:
                c_b, b_b = cg.astype(BF16), bg.astype(BF16)
            term = _mm_nt(c_b, hprev.astype(BF16))
            tick()
            acc = term if acc is None else acc + term
            dec = jnp.exp(tot[b * rows_per_seq:b * rows_per_seq + 1, :])
            scaled = [hprev[e * HEAD_DIM:(e + 1) * HEAD_DIM, :]
                      * dec[:, g * HEADS_PER_GROUP + e:g * HEADS_PER_GROUP + e + 1]
                      for e in range(HEADS_PER_GROUP)]
            upd = _mm(xw_t[g * GROUP_W:(g + 1) * GROUP_W, :], b_b)
            tick()
            h_set(b, g, jnp.concatenate(scaled, axis=0) + upd)
        yoff.append(acc)

    y = part["base"] + jnp.concatenate(yoff, axis=1) * part["seg_e"]
    yz = y * (z * _sigmoid(z))
    outs = []
    for g in range(N_GROUPS):
        blk = yz[:, g * GROUP_W:(g + 1) * GROUP_W]
        ms = jnp.mean(blk * blk, axis=-1, keepdims=True)
        outs.append(blk * lax.rsqrt(ms + EPS) * snorm_g[:, g * GROUP_W:(g + 1) * GROUP_W])
    return jnp.concatenate(outs, axis=1)


def _dt_and_a(dt_raw, dtb, alog, pad_rows, valid_rows_per_tile=None):
    v = dt_raw + dtb
    dt = jnp.maximum(v, 0.0) + jnp.log1p(jnp.exp(-jnp.abs(v)))
    lane = lax.broadcasted_iota(jnp.int32, dt.shape, 1)
    keep = lane < N_HEADS
    row = lax.broadcasted_iota(jnp.int32, dt.shape, 0)
    if pad_rows:
        keep = jnp.logical_and(row >= pad_rows, keep)
    if valid_rows_per_tile is not None:
        keep = jnp.logical_and((row & (SAMPLE_ROWS - 1)) < valid_rows_per_tile, keep)
    dt = jnp.where(keep, dt, 0.0)
    return dt, dt * (-jnp.exp(alog))


def _layernorm_silu(cv, g, b):
    mu = jnp.mean(cv, axis=-1, keepdims=True)
    d = cv - mu
    var = jnp.mean(d * d, axis=-1, keepdims=True)
    cn = d * lax.rsqrt(var + EPS) * g + b
    return cn * _sigmoid(cn)


CONV_ROWS = 32
XCONV_ROWS = 64
CONF_PAD = 32
XBC_PAD = 8
PROJ_COLS = 256


def _lane_blocks(width):
    return width // LANES


def _mixer_p_body(pad_rows, nchunk, *refs):
    s = pl.program_id(0)
    nshared = 4
    sets = refs[-nshared - 8:-nshared]

    @pl.when(s == 0)
    def _():
        for ref in sets:
            ref[...] = jnp.zeros(ref.shape, F32)

    for slot_p in range(2):
        @pl.when(lax.rem(s, 2) == slot_p)
        def _(slot_p=slot_p):
            _mixer_p_step(slot_p, pad_rows, nchunk, *refs)


def _mixer_p_step(slot_p, pad_rows, nchunk, flag_ref, xp_ref, xm_ref, gm_ref, win_ref, wdt_ref, conf0_ref, xbc0_ref,
                  ssm0_ref, cw_ref, cb_ref, lng_ref, lnb_ref, sw_ref, sb_ref, dtb_ref, alog_ref,
                  dexp_ref, sng_ref, e2_ref, wout_ref, h1_ref, conf_out_ref, xbc_out_ref,
                  ssm_out_ref, gbuf0, xbuf0, zbuf0, dtbuf0, gbuf1, xbuf1, zbuf1, dtbuf1,
                  hst, cbuf, xact, cy):
    s = pl.program_id(0)
    m = jnp.maximum(s - 1, 0)
    c = lax.rem(m, nchunk)
    last = nchunk - 1
    tile = xp_ref.shape[0]
    conf_lo = CONF_PAD - (CONF_K - 1)
    xbc_lo = XBC_PAD - (SSM_K - 1)
    sets = ((gbuf0, xbuf0, zbuf0, dtbuf0), (gbuf1, xbuf1, zbuf1, dtbuf1))
    gnext, xnext, znext, dtnext = sets[slot_p]
    gbuf, xbuf, zbuf, dtbuf = sets[1 - slot_p]

    @pl.when(c == 0)
    def _():
        for j in range(_lane_blocks(D_CONV)):
            gbuf[j, conf_lo:CONF_PAD, :] = conf0_ref[0, :, j * LANES:(j + 1) * LANES]
        for j in range(_lane_blocks(XBC)):
            xbuf[j, xbc_lo:XBC_PAD, :] = xbc0_ref[0, :, j * LANES:(j + 1) * LANES]

    u = _rms(xp_ref[...], gm_ref[...]).astype(BF16)
    per = PROJ_COLS // LANES

    def proj_glu(i):
        lin = _mm(u, win_ref[:, i * PROJ_COLS:(i + 1) * PROJ_COLS])
        gate = _mm(u, win_ref[:, OFF_GATE + i * PROJ_COLS:OFF_GATE + (i + 1) * PROJ_COLS])
        glu = lin * _sigmoid(gate)
        for jj in range(per):
            gnext[i * per + jj, CONF_PAD:CONF_PAD + tile, :] = glu[:, jj * LANES:(jj + 1) * LANES]

    def proj_z(i):
        znext[:, i * PROJ_COLS:(i + 1) * PROJ_COLS] = _mm(
            u, win_ref[:, OFF_Z + i * PROJ_COLS:OFF_Z + (i + 1) * PROJ_COLS])

    def proj_xbc(i):
        res = _mm(u, win_ref[:, OFF_XBC + i * PROJ_COLS:OFF_XBC + (i + 1) * PROJ_COLS])
        for jj in range(per):
            xnext[i * per + jj, XBC_PAD:XBC_PAD + tile, :] = res[:, jj * LANES:(jj + 1) * LANES]

    def proj_dt():
        dtnext[...] = _mm(u, wdt_ref[...])

    stage1 = ([(2, functools.partial(proj_glu, i)) for i in range(D_CONV // PROJ_COLS)]
              + [(1, functools.partial(proj_z, i)) for i in range(D_SSM // PROJ_COLS)]
              + [(1, functools.partial(proj_xbc, i)) for i in range(XBC // PROJ_COLS)]
              + [(1, proj_dt)])

    x = xm_ref[...]
    dt_raw = dtbuf[...]

    chain = []

    def conv_start(bias, rows):
        init = jnp.broadcast_to(bias, (rows, LANES))
        if chain:
            never = jnp.broadcast_to(flag_ref[0], (rows, LANES)) != 0
            init = jnp.where(never, chain[-1][0:rows, :], init)
        return init

    def conv_conf(rb, j):
        cols = slice(j * LANES, (j + 1) * LANES)
        acc = conv_start(cb_ref[:, cols], CONV_ROWS)
        for k in range(CONF_K):
            r0 = rb * CONV_ROWS + conf_lo + k
            acc = acc + cw_ref[k:k + 1, cols] * gbuf[j, r0:r0 + CONV_ROWS, :]
        cbuf[rb * CONV_ROWS:(rb + 1) * CONV_ROWS, cols] = acc
        chain.append(acc)

    def conv_xbc(rb, j):
        cols = slice(j * LANES, (j + 1) * LANES)
        acc = conv_start(sb_ref[:, cols], XCONV_ROWS)
        for k in range(SSM_K):
            r0 = rb * XCONV_ROWS + xbc_lo + k
            acc = acc + sw_ref[k:k + 1, cols] * xbuf[j, r0:r0 + XCONV_ROWS, :]
        xact[rb * XCONV_ROWS:(rb + 1) * XCONV_ROWS, cols] = acc * _sigmoid(acc)
        chain.append(acc)

    for _, item in stage1:
        item()
    assert XCONV_ROWS >= CONV_ROWS
    for rb in range(tile // XCONV_ROWS):
        for j in range(_lane_blocks(XBC)):
            conv_xbc(rb, j)
    for rb in range(tile // CONV_ROWS):
        for j in range(_lane_blocks(D_CONV)):
            conv_conf(rb, j)

    @pl.when(c == 0)
    def _():
        hst[...] = ssm0_ref[0].reshape(N_HEADS * HEAD_DIM, D_STATE)

    cy[:, 0:D_CONV] = _layernorm_silu(cbuf[...], lng_ref[...], lnb_ref[...]).astype(BF16)
    gnext[:, 0:CONF_PAD, :] = gbuf[:, tile:tile + CONF_PAD, :]
    xnext[:, 0:XBC_PAD, :] = xbuf[:, tile:tile + XBC_PAD, :]

    dt, a = _dt_and_a(dt_raw, dtb_ref[...], alog_ref[...], pad_rows)

    def h_get(b, g):
        return hst[g * GROUP_W:(g + 1) * GROUP_W, :]

    def h_set(b, g, val):
        hst[g * GROUP_W:(g + 1) * GROUP_W, :] = val

    for j in range(tile // SSD_CHUNK):
        rs = slice(j * SSD_CHUNK, (j + 1) * SSD_CHUNK)
        part = _ssd_intra(xact[rs, 0:D_SSM], xact[rs, D_SSM:D_SSM + GN], xact[rs, D_SSM + GN:XBC],
                          dt[rs], a[rs], SSD_CHUNK, e2_ref[...], dexp_ref[...])
        out = _ssd_inter(part, zbuf[rs, :], SSD_CHUNK, sng_ref[...], h_get, h_set)
        cy[rs, D_CONV:2 * D_CONV] = out.astype(BF16)
        h1_ref[rs, :] = x[rs, :] + _mm(cy[rs, :], wout_ref[...])

    @pl.when(c == last)
    def _():
        for j in range(_lane_blocks(D_CONV)):
            conf_out_ref[0, :, j * LANES:(j + 1) * LANES] = gbuf[j, tile + conf_lo:tile + CONF_PAD, :]
        for j in range(_lane_blocks(XBC)):
            xbc_out_ref[0, :, j * LANES:(j + 1) * LANES] = xbuf[j, tile + xbc_lo:tile + XBC_PAD, :]
        ssm_out_ref[0] = hst[...].reshape(N_HEADS, HEAD_DIM, D_STATE)


def _mixer_p(x, conf0, xbc0, ssm0, p, nbatch, tile, pad_rows):
    rows = x.shape[0]
    ntile = rows // tile
    nchunk = ntile // nbatch
    stage1_map = lambda s: (jnp.minimum(s, ntile - 1), 0)
    stage2_map = lambda s: (jnp.maximum(s - 1, 0), 0)
    state_specs = lambda shp: pl.BlockSpec(
        (1,) + shp, lambda s: (jnp.maximum(s - 1, 0) // nchunk,) + (0,) * len(shp))
    small = [p["conf_w"], p["conf_b"], p["ln_g"], p["ln_b"], p["sconv_w"], p["sconv_b"],
             p["dt_bias"], p["a_log"], p["d_exp"], p["snorm_g"], p["e2"]]
    return pl.pallas_call(
        functools.partial(_mixer_p_body, pad_rows, nchunk),
        out_shape=(jax.ShapeDtypeStruct((rows, D_MODEL), F32),
                   jax.ShapeDtypeStruct((nbatch, CONF_K - 1, D_CONV), F32),
                   jax.ShapeDtypeStruct((nbatch, SSM_K - 1, XBC), F32),
                   jax.ShapeDtypeStruct((nbatch, N_HEADS, HEAD_DIM, D_STATE), F32)),
        grid=(ntile + 1,),
        in_specs=[pl.BlockSpec(memory_space=pltpu.SMEM),
                  pl.BlockSpec((tile, D_MODEL), stage1_map),
                  pl.BlockSpec((tile, D_MODEL), stage2_map),
                  _const_spec((1, D_MODEL)),
                  _const_spec((D_MODEL, OFF_DT)),
                  _const_spec((D_MODEL, LANES)),
                  _const_spec((1, CONF_K - 1, D_CONV)),
                  _const_spec((1, SSM_K - 1, XBC)),
                  _const_spec((1, N_HEADS, HEAD_DIM, D_STATE))]
                 + [_const_spec(s.shape) for s in small]
                 + [_const_spec((2 * D_CONV, D_MODEL))],
        out_specs=(pl.BlockSpec((tile, D_MODEL), stage2_map),
                   state_specs((CONF_K - 1, D_CONV)),
                   state_specs((SSM_K - 1, XBC)),
                   state_specs((N_HEADS, HEAD_DIM, D_STATE))),
        scratch_shapes=2 * [pltpu.VMEM((_lane_blocks(D_CONV), tile + CONF_PAD, LANES), F32),
                            pltpu.VMEM((_lane_blocks(XBC), tile + XBC_PAD, LANES), F32),
                            pltpu.VMEM((tile, D_SSM), F32),
                            pltpu.VMEM((tile, LANES), F32)]
                       + [pltpu.VMEM((N_HEADS * HEAD_DIM, D_STATE), F32),
                        pltpu.VMEM((tile, D_CONV), F32),
                        pltpu.VMEM((tile, XBC), F32),
                        pltpu.VMEM((tile, 2 * D_CONV), BF16)],
        compiler_params=_params(48),
        name="mixer_prompt",
    )(jnp.zeros((1,), jnp.int32), x, x, p["g_mix"], p["w_in"], p["w_dt"], conf0, xbc0, ssm0,
      *small, p["w_out"])


def _mix_s_body(proj_ref, conf_ref, xbc_ref, ssm_ref, cw_ref, cb_ref, lng_ref, lnb_ref, sw_ref,
                sb_ref, dtb_ref, alog_ref, dexp_ref, sng_ref, e2_ref,
                cy_ref, conf_out_ref, xbc_out_ref, ssm_out_ref, fbuf, xfb):
    nseq = conf_ref.shape[0]
    rows = nseq * SAMPLE_ROWS
    new_rows = proj_ref.shape[0] // nseq // 2
    conf_lo = CONF_PAD - (CONF_K - 1)
    xbc_lo = XBC_PAD - (SSM_K - 1)

    glu = proj_ref[:, 0:D_CONV] * _sigmoid(proj_ref[:, OFF_GATE:OFF_Z])
    fbuf[:, conf_lo:CONF_PAD, :] = conf_ref[...]
    fbuf[:, CONF_PAD:CONF_PAD + SAMPLE_ROWS, :] = glu.reshape(nseq, SAMPLE_ROWS, D_CONV)
    acc = jnp.broadcast_to(cb_ref[...].reshape(1, 1, D_CONV), (nseq, SAMPLE_ROWS, D_CONV))
    for k in range(CONF_K):
        w = cw_ref[k:k + 1, :].reshape(1, 1, D_CONV)
        acc = acc + w * fbuf[:, conf_lo + k:conf_lo + k + SAMPLE_ROWS, :]
    conf_out_ref[...] = fbuf[:, conf_lo + new_rows:CONF_PAD + new_rows, :]
    cy_ref[:, 0:D_CONV] = _layernorm_silu(acc.reshape(rows, D_CONV), lng_ref[...],
                                          lnb_ref[...]).astype(BF16)

    xfb[:, xbc_lo:XBC_PAD, :] = xbc_ref[...]
    xfb[:, XBC_PAD:XBC_PAD + SAMPLE_ROWS, :] = proj_ref[:, OFF_XBC:OFF_DT].reshape(
        nseq, SAMPLE_ROWS, XBC)
    xv = jnp.broadcast_to(sb_ref[...].reshape(1, 1, XBC), (nseq, SAMPLE_ROWS, XBC))
    for k in range(SSM_K):
        w = sw_ref[k:k + 1, :].reshape(1, 1, XBC)
        xv = xv + w * xfb[:, xbc_lo + k:xbc_lo + k + SAMPLE_ROWS, :]
    xbc_out_ref[...] = xfb[:, xbc_lo + new_rows:XBC_PAD + new_rows, :]
    xv = xv.reshape(rows, XBC)
    xv = xv * _sigmoid(xv)

    dt, a = _dt_and_a(proj_ref[:, OFF_DT:PROJ_W], dtb_ref[...], alog_ref[...], 0, new_rows)

    def h_get(b, g):
        return ssm_ref[b, g * HEADS_PER_GROUP:(g + 1) * HEADS_PER_GROUP].reshape(GROUP_W, D_STATE)

    def h_set(b, g, val):
        ssm_out_ref[b, g * HEADS_PER_GROUP:(g + 1) * HEADS_PER_GROUP] = val.reshape(
            HEADS_PER_GROUP, HEAD_DIM, D_STATE)

    part = _ssd_intra(xv[:, 0:D_SSM], xv[:, D_SSM:D_SSM + GN], xv[:, D_SSM + GN:XBC], dt, a,
                      SAMPLE_ROWS, e2_ref[...], dexp_ref[...])
    out = _ssd_inter(part, proj_ref[:, OFF_Z:OFF_XBC], SAMPLE_ROWS, sng_ref[...], h_get, h_set)
    cy_ref[:, D_CONV:2 * D_CONV] = out.astype(BF16)


def _mix_s(proj, conf, xbc, ssm, p):
    nbatch = conf.shape[0]
    rows = SAMPLE_SEQS * SAMPLE_ROWS
    small = [p["conf_w"], p["conf_b"], p["ln_g"], p["ln_b"], p["sconv_w"], p["sconv_b"],
             p["dt_bias"], p["a_log"], p["d_exp"], p["snorm_g"], p["e2"]]
    blk3 = lambda shp: pl.BlockSpec((SAMPLE_SEQS,) + shp, lambda i: (i,) + (0,) * len(shp))
    return pl.pallas_call(
        _mix_s_body,
        out_shape=(jax.ShapeDtypeStruct((nbatch * SAMPLE_ROWS, 2 * D_CONV), BF16),
                   jax.ShapeDtypeStruct(conf.shape, F32),
                   jax.ShapeDtypeStruct(xbc.shape, F32),
                   jax.ShapeDtypeStruct(ssm.shape, F32)),
        grid=(nbatch // SAMPLE_SEQS,),
        in_specs=[pl.BlockSpec((rows, PROJ_W), lambda i: (i, 0)),
                  blk3((CONF_K - 1, D_CONV)), blk3((SSM_K - 1, XBC)),
                  blk3((N_HEADS, HEAD_DIM, D_STATE))]
                 + [_const_spec(s.shape) for s in small],
        out_specs=(pl.BlockSpec((rows, 2 * D_CONV), lambda i: (i, 0)),
                   blk3((CONF_K - 1, D_CONV)), blk3((SSM_K - 1, XBC)),
                   blk3((N_HEADS, HEAD_DIM, D_STATE))),
        scratch_shapes=[pltpu.VMEM((SAMPLE_SEQS, CONF_PAD + SAMPLE_ROWS, D_CONV), F32),
                        pltpu.VMEM((SAMPLE_SEQS, XBC_PAD + SAMPLE_ROWS, XBC), F32)],
        compiler_params=_params(40),
        name="mix_sample",
    )(proj, conf, xbc, ssm, *small)


FFN_COLS = 256
FFN_PAD = 8


def _ffn_p_body(nchunk, *refs):
    s = pl.program_id(0)
    bufs = refs[-3:-1]

    @pl.when(s == 0)
    def _():
        for ref in bufs:
            ref[...] = jnp.zeros(ref.shape, F32)

    for slot_p in range(2):
        @pl.when(lax.rem(s, 2) == slot_p)
        def _(slot_p=slot_p):
            _ffn_p_step(slot_p, nchunk, *refs)


def _ffn_p_step(slot_p, nchunk, flag_ref, hp_ref, hm_ref, gn_ref, wu_ref, tail0_ref, cw_ref, cb_ref,
                wd_ref, gf_ref, y_ref, tail_out_ref, ubuf0, ubuf1, act):
    s = pl.program_id(0)
    c = lax.rem(jnp.maximum(s - 1, 0), nchunk)
    tile = hp_ref.shape[0]
    lo = FFN_PAD - (FFN_K - 1)
    nblk = _lane_blocks(2 * D_FF)
    half = _lane_blocks(D_FF)
    unext = (ubuf0, ubuf1)[slot_p]
    ubuf = (ubuf0, ubuf1)[1 - slot_p]

    @pl.when(c == 0)
    def _():
        for j in range(nblk):
            ubuf[j, lo:FFN_PAD, :] = tail0_ref[0, :, j * LANES:(j + 1) * LANES]

    u = _rms(hp_ref[...], gn_ref[...]).astype(BF16)
    per = PROJ_COLS // LANES
    for i in range(2 * D_FF // PROJ_COLS):
        res = _mm(u, wu_ref[:, i * PROJ_COLS:(i + 1) * PROJ_COLS])
        for jj in range(per):
            unext[i * per + jj, FFN_PAD:FFN_PAD + tile, :] = res[:, jj * LANES:(jj + 1) * LANES]

    def conv(j, r0, init):
        cols = slice(j * LANES, (j + 1) * LANES)
        out = init
        for k in range(FFN_K):
            out = out + cw_ref[k:k + 1, cols] * ubuf[j, r0 + lo + k:r0 + lo + k + XCONV_ROWS, :]
        return out

    def bias(j):
        return jnp.broadcast_to(cb_ref[:, j * LANES:(j + 1) * LANES], (XCONV_ROWS, LANES))

    prev = None
    for rb in range(tile // XCONV_ROWS):
        r0 = rb * XCONV_ROWS
        for j in range(half):
            init = bias(j)
            if prev is not None:
                never = jnp.broadcast_to(flag_ref[0], (XCONV_ROWS, LANES)) != 0
                init = jnp.where(never, prev, init)
            gate = conv(j, r0, init)
            prev = gate * _sigmoid(gate) * conv(half + j, r0, bias(half + j))
            act[r0:r0 + XCONV_ROWS, j * LANES:(j + 1) * LANES] = prev.astype(BF16)

    @pl.when(c == nchunk - 1)
    def _():
        for j in range(nblk):
            tail_out_ref[0, :, j * LANES:(j + 1) * LANES] = ubuf[j, tile + lo:tile + FFN_PAD, :]

    unext[:, 0:FFN_PAD, :] = ubuf[:, tile:tile + FFN_PAD, :]
    y_ref[...] = _rms(hm_ref[...] + _mm(act[...], wd_ref[...]), gf_ref[...])


def _ffn_p(h1, tail0, p, nbatch, tile):
    rows = h1.shape[0]
    ntile = rows // tile
    nchunk = ntile // nbatch
    stage1_map = lambda s: (jnp.minimum(s, ntile - 1), 0)
    stage2_map = lambda s: (jnp.maximum(s - 1, 0), 0)
    return pl.pallas_call(
        functools.partial(_ffn_p_body, nchunk),
        out_shape=(jax.ShapeDtypeStruct((rows, D_MODEL), F32),
                   jax.ShapeDtypeStruct((nbatch, FFN_K - 1, 2 * D_FF), F32)),
        grid=(ntile + 1,),
        in_specs=[pl.BlockSpec(memory_space=pltpu.SMEM),
                  pl.BlockSpec((tile, D_MODEL), stage1_map),
                  pl.BlockSpec((tile, D_MODEL), stage2_map),
                  _const_spec((1, D_MODEL)),
                  _const_spec((D_MODEL, 2 * D_FF)),
                  _const_spec((1, FFN_K - 1, 2 * D_FF)),
                  _const_spec((FFN_K, 2 * D_FF)),
                  _const_spec((1, 2 * D_FF)),
                  _const_spec((D_FF, D_MODEL)),
                  _const_spec((1, D_MODEL))],
        out_specs=(pl.BlockSpec((tile, D_MODEL), stage2_map),
                   pl.BlockSpec((1, FFN_K - 1, 2 * D_FF),
                                lambda s: (jnp.maximum(s - 1, 0) // nchunk, 0, 0))),
        scratch_shapes=2 * [pltpu.VMEM((_lane_blocks(2 * D_FF), tile + FFN_PAD, LANES), F32)]
                       + [pltpu.VMEM((tile, D_FF), BF16)],
        compiler_params=_params(48),
        name="ffn_prompt",
    )(jnp.zeros((1,), jnp.int32), h1, h1, p["g_ffn"], p["w_up"], tail0, p["fconv_w"],
      p["fconv_b"], p["w_down"], p["g_final"])


def _ffn_s_body(up_ref, h1_ref, st_ref, cw_ref, cb_ref, wd_ref, gf_ref,
                y_ref, st_out_ref, ubuf, act):
    nseq = st_ref.shape[0]
    rows = nseq * SAMPLE_ROWS
    new_rows = SAMPLE_ROWS // 2
    lo = FFN_PAD - (FFN_K - 1)
    ubuf[:, lo:FFN_PAD, :] = st_ref[...]
    ubuf[:, FFN_PAD:FFN_PAD + SAMPLE_ROWS, :] = up_ref[...].reshape(nseq, SAMPLE_ROWS, 2 * D_FF)

    def conv(cols):
        out = jnp.broadcast_to(cb_ref[:, cols].reshape(1, 1, FFN_COLS),
                               (nseq, SAMPLE_ROWS, FFN_COLS))
        for k in range(FFN_K):
            w = cw_ref[k:k + 1, cols].reshape(1, 1, FFN_COLS)
            out = out + w * ubuf[:, lo + k:lo + k + SAMPLE_ROWS, cols]
        return out.reshape(rows, FFN_COLS)

    for j in range(D_FF // FFN_COLS):
        gate = conv(slice(j * FFN_COLS, (j + 1) * FFN_COLS))
        lin = conv(slice(D_FF + j * FFN_COLS, D_FF + (j + 1) * FFN_COLS))
        act[:, j * FFN_COLS:(j + 1) * FFN_COLS] = (gate * _sigmoid(gate) * lin).astype(BF16)

    st_out_ref[...] = ubuf[:, lo + new_rows:FFN_PAD + new_rows, :]
    y_ref[...] = _rms(h1_ref[...] + _mm(act[...], wd_ref[...]), gf_ref[...])


def _ffn_s(up, h1, st, p):
    nbatch = st.shape[0]
    rows = SAMPLE_SEQS * SAMPLE_ROWS
    return pl.pallas_call(
        _ffn_s_body,
        out_shape=(jax.ShapeDtypeStruct((nbatch * SAMPLE_ROWS, D_MODEL), F32),
                   jax.ShapeDtypeStruct(st.shape, F32)),
        grid=(nbatch // SAMPLE_SEQS,),
        in_specs=[pl.BlockSpec((rows, 2 * D_FF), lambda i: (i, 0)),
                  pl.BlockSpec((rows, D_MODEL), lambda i: (i, 0)),
                  pl.BlockSpec((SAMPLE_SEQS, FFN_K - 1, 2 * D_FF), lambda i: (i, 0, 0)),
                  _const_spec((FFN_K, 2 * D_FF)),
                  _const_spec((1, 2 * D_FF)),
                  _const_spec((D_FF, D_MODEL)),
                  _const_spec((1, D_MODEL))],
        out_specs=(pl.BlockSpec((rows, D_MODEL), lambda i: (i, 0)),
                   pl.BlockSpec((SAMPLE_SEQS, FFN_K - 1, 2 * D_FF), lambda i: (i, 0, 0))),
        scratch_shapes=[pltpu.VMEM((SAMPLE_SEQS, FFN_PAD + SAMPLE_ROWS, 2 * D_FF), F32),
                        pltpu.VMEM((rows, D_FF), BF16)],
        compiler_params=_params(40),
        name="ffn_sample",
    )(up, h1, st, p["fconv_w"], p["fconv_b"], p["w_down"], p["g_final"])


def _head_expander():
    k = lax.broadcasted_iota(jnp.int32, (2 * LANES, D_SSM), 0) % LANES
    j = lax.broadcasted_iota(jnp.int32, (2 * LANES, D_SSM), 1) // HEAD_DIM
    return (k == j).astype(BF16)


def _lane_pad(v):
    return jnp.pad(v.astype(F32), (0, LANES - v.shape[0])).reshape(1, LANES)


def kernel(x_prompt, x_sample, state_conf_conv, state_xbc_conv, state_ssm, state_ffn_conv, meta_tokens, norm_mix_g, w_in, conf_conv_w, conf_conv_b, conf_ln_g, conf_ln_b, ssm_conv_w, ssm_conv_b, dt_bias, a_log, d_skip, ssm_norm_g, w_out, norm_ffn_g, w_up, ffn_conv_w, ffn_conv_b, w_down, norm_final_g):
    depth = w_in.shape[0]
    assert depth == 1, "one layer per step"
    nb, seq, _ = x_prompt.shape
    nsb, sseq, _ = x_sample.shape
    assert 2 * sseq == SAMPLE_ROWS and nsb % SAMPLE_SEQS == 0 and N_META <= SSD_CHUNK
    row = lambda v: v.astype(F32).reshape(1, -1)
    p = dict(
        conf_w=conf_conv_w[0], conf_b=row(conf_conv_b[0]), ln_g=row(conf_ln_g[0]),
        ln_b=row(conf_ln_b[0]), sconv_w=ssm_conv_w[0], sconv_b=row(ssm_conv_b[0]),
        dt_bias=_lane_pad(dt_bias[0]), a_log=_lane_pad(a_log[0]),
        d_exp=row(jnp.repeat(d_skip[0], HEAD_DIM)), snorm_g=row(ssm_norm_g[0]),
        e2=_head_expander(), fconv_w=ffn_conv_w[0], fconv_b=row(ffn_conv_b[0]),
        w_down=w_down[0].astype(BF16), g_final=row(norm_final_g),
        g_mix=row(norm_mix_g[0]), g_ffn=row(norm_ffn_g[0]),
        w_in=w_in[0, :, 0:OFF_DT].astype(BF16),
        w_dt=jnp.pad(w_in[0, :, OFF_DT:], ((0, 0), (0, PROJ_W - w_in.shape[2]))).astype(BF16),
        w_out=w_out[0].astype(BF16), w_up=w_up[0].astype(BF16))
    g_mix, g_ffn, w_out_b, w_up_b = p["g_mix"], p["g_ffn"], p["w_out"], p["w_up"]

    meta = jnp.concatenate([jnp.zeros((SSD_CHUNK - N_META, D_MODEL), F32),
                            meta_tokens.astype(F32)], axis=0)
    h1_m, conf0, xbc0, ssm0 = _mixer_p(
        meta, jnp.zeros((1, CONF_K - 1, D_CONV), F32), jnp.zeros((1, SSM_K - 1, XBC), F32),
        jnp.zeros((1, N_HEADS, HEAD_DIM, D_STATE), F32), p, 1, SSD_CHUNK, SSD_CHUNK - N_META)
    _, ffn0 = _ffn_p(h1_m, jnp.zeros((1, FFN_K - 1, 2 * D_FF), F32), p, 1, SSD_CHUNK)

    tile = 256
    xp = x_prompt.reshape(nb * seq, D_MODEL)
    h1_p, p_conf, p_xbc, p_ssm = _mixer_p(xp, conf0, xbc0, ssm0, p, nb, tile, 0)
    y_p, p_ffn = _ffn_p(h1_p, ffn0, p, nb, tile)

    xs = jnp.pad(x_sample, ((0, 0), (0, SAMPLE_ROWS - sseq), (0, 0))).reshape(
        nsb * SAMPLE_ROWS, D_MODEL)
    proj_s = _in_proj(xs, g_mix, p["w_in"], p["w_dt"], tile)
    cy_s, s_conf, s_xbc, s_ssm = _mix_s(proj_s, state_conf_conv[0], state_xbc_conv[0],
                                        state_ssm[0], p)
    h1_s, up_s = _out_up(cy_s, xs, w_out_b, g_ffn, w_up_b, tile)
    y_s, s_ffn = _ffn_s(up_s, h1_s, state_ffn_conv[0], p)

    y_prompt = y_p.reshape(nb, seq, D_MODEL)
    y_sample = y_s.reshape(nsb, SAMPLE_ROWS, D_MODEL)[:, :sseq]
    return (y_prompt, y_sample, p_conf[None], p_xbc[None], p_ssm[None], p_ffn[None],
            s_conf[None], s_xbc[None], s_ssm[None], s_ffn[None])
```

```python
import functools

import jax
import jax.numpy as jnp
from jax import lax
from jax.experimental import pallas as pl
from jax.experimental.pallas import tpu as pltpu

F32 = jnp.float32
BF16 = jnp.bfloat16

D_MODEL = 1024
N_META = 16
D_CONV = 1024
D_SSM = 1024
HEAD_DIM = 64
N_HEADS = D_SSM // HEAD_DIM
N_GROUPS = 2
HEADS_PER_GROUP = N_HEADS // N_GROUPS
D_STATE = 128
GN = N_GROUPS * D_STATE
GROUP_W = D_SSM // N_GROUPS
XBC = D_SSM + 2 * GN
D_FF = 2816
CONF_K = 31
SSM_K = 4
FFN_K = 3
OFF_GATE = D_CONV
OFF_Z = 2 * D_CONV
OFF_XBC = OFF_Z + D_SSM
OFF_DT = OFF_XBC + XBC
LANES = 128
SUBLANES = 8
PROJ_W = OFF_DT + LANES
EPS = 1e-5

SSD_CHUNK = 128
SAMPLE_ROWS = 8
SAMPLE_SEQS = 8
MIB = 1024 * 1024


def _sigmoid(x):
    return 1.0 / (1.0 + jnp.exp(-x))


def _rms(x, g):
    return x * lax.rsqrt(jnp.mean(x * x, axis=-1, keepdims=True) + EPS) * g


def _mm(a, b):
    return jnp.dot(a, b, preferred_element_type=F32)


def _mm_nt(a, b):
    return lax.dot_general(a, b, (((1,), (1,)), ((), ())), preferred_element_type=F32)


def _const_spec(shape):
    zeros = (0,) * len(shape)
    return pl.BlockSpec(shape, lambda *_: zeros, pipeline_mode=pl.Buffered(1))


def _params(vmem_mib, ndims=1, flags=None):
    return pltpu.CompilerParams(dimension_semantics=("arbitrary",) * ndims,
                                vmem_limit_bytes=vmem_mib * MIB, flags=flags)


def _in_proj_body(x_ref, g_ref, w_ref, wdt_ref, o_ref):
    u = _rms(x_ref[...], g_ref[...]).astype(BF16)
    o_ref[:, 0:OFF_DT] = _mm(u, w_ref[...])
    o_ref[:, OFF_DT:PROJ_W] = _mm(u, wdt_ref[...])


def _in_proj(x, g, w, w_dt, row_tile):
    rows = x.shape[0]
    return pl.pallas_call(
        _in_proj_body,
        out_shape=jax.ShapeDtypeStruct((rows, PROJ_W), F32),
        grid=(rows // row_tile,),
        in_specs=[pl.BlockSpec((row_tile, D_MODEL), lambda i: (i, 0)),
                  _const_spec((1, D_MODEL)),
                  _const_spec((D_MODEL, OFF_DT)),
                  _const_spec((D_MODEL, LANES))],
        out_specs=pl.BlockSpec((row_tile, PROJ_W), lambda i: (i, 0)),
        compiler_params=_params(44),
        name="in_proj",
    )(x, g, w, w_dt)


def _out_up_body(cy_ref, h_ref, wo_ref, g_ref, wu_ref, h1_ref, up_ref):
    h1 = h_ref[...] + _mm(cy_ref[...], wo_ref[...])
    h1_ref[...] = h1
    up_ref[...] = _mm(_rms(h1, g_ref[...]).astype(BF16), wu_ref[...])


def _out_up(cy, h, w_out, g, w_up, row_tile):
    rows = h.shape[0]
    return pl.pallas_call(
        _out_up_body,
        out_shape=(jax.ShapeDtypeStruct((rows, D_MODEL), F32),
                   jax.ShapeDtypeStruct((rows, 2 * D_FF), F32)),
        grid=(rows // row_tile,),
        in_specs=[pl.BlockSpec((row_tile, 2 * D_CONV), lambda i: (i, 0)),
                  pl.BlockSpec((row_tile, D_MODEL), lambda i: (i, 0)),
                  _const_spec((2 * D_CONV, D_MODEL)),
                  _const_spec((1, D_MODEL)),
                  _const_spec((D_MODEL, 2 * D_FF))],
        out_specs=(pl.BlockSpec((row_tile, D_MODEL), lambda i: (i, 0)),
                   pl.BlockSpec((row_tile, 2 * D_FF), lambda i: (i, 0))),
        compiler_params=_params(52),
        name="out_up",
    )(cy, h, w_out, g, w_up)


def _split3(v):
    v1 = v.astype(BF16)
    r = v - v1.astype(F32)
    v2 = r.astype(BF16)
    v3 = (r - v2.astype(F32)).astype(BF16)
    return v1, v2, v3


def _ssd_intra(xs, bm, cm, dt, a, rows_per_seq, e2, d_exp, tick=lambda: None):
    tile = xs.shape[0]
    nseq = tile // rows_per_seq
    shift = rows_per_seq.bit_length() - 1
    row = lax.broadcasted_iota(jnp.int32, (tile, tile), 0)
    col = lax.broadcasted_iota(jnp.int32, (tile, tile), 1)
    causal = col <= row
    if nseq > 1:
        same = lax.shift_right_logical(row, shift) == lax.shift_right_logical(col, shift)
        mask = jnp.logical_and(same, causal)
        bd = jnp.where(same, 1.0, 0.0).astype(BF16)
    else:
        mask = causal
        bd = jnp.ones((tile, tile), BF16)
    tri = jnp.where(mask, 1.0, 0.0).astype(BF16)

    def mm(p, q, nt=False):
        out = _mm_nt(p, q) if nt else _mm(p, q)
        tick()
        return out

    a1, a2, a3 = _split3(a)
    acs = mm(tri, a1) + mm(tri, a2) + mm(tri, a3)
    tot = mm(bd, a1) + mm(bd, a2) + mm(bd, a3)
    acs_t = acs.T

    def expand(q):
        q1 = q.astype(BF16)
        q2 = (q - q1.astype(F32)).astype(BF16)
        return mm(jnp.concatenate([q1, q2], axis=1), e2)

    xdt = xs * expand(dt)
    xw_t = (xs * expand(dt * jnp.exp(tot - acs))).T.astype(BF16)
    seg_e = expand(jnp.exp(acs))

    lane = lax.broadcasted_iota(jnp.int32, (tile, LANES), 1)
    low_half = lane < HEAD_DIM
    ydiag = []
    for g in range(N_GROUPS):
        cgb = cm[:, g * D_STATE:(g + 1) * D_STATE].astype(BF16)
        bgb = bm[:, g * D_STATE:(g + 1) * D_STATE].astype(BF16)
        cb = mm(cgb, bgb, nt=True)
        for pair in range(HEADS_PER_GROUP // 2):
            blk = g * (HEADS_PER_GROUP // 2) + pair
            xb = xdt[:, blk * LANES:(blk + 1) * LANES]
            acc = None
            for half in range(2):
                h = 2 * blk + half
                diff = acs[:, h:h + 1] - acs_t[h:h + 1, :]
                decay = jnp.where(mask, jnp.exp(jnp.where(mask, diff, 0.0)), 0.0)
                xh = jnp.where(low_half if half == 0 else jnp.logical_not(low_half), xb, 0.0)
                term = mm((cb * decay).astype(BF16), xh.astype(BF16))
                acc = term if acc is None else acc + term
            ydiag.append(acc)
    return dict(base=jnp.concatenate(ydiag, axis=1) + xs * d_exp, xw_t=xw_t, seg_e=seg_e,
                tot=tot, bm=bm, cm=cm)


def _ssd_inter(part, z, rows_per_seq, snorm_g, h_get, h_set, tick=lambda: None):
    bm, cm, tot, xw_t = part["bm"], part["cm"], part["tot"], part["xw_t"]
    tile = bm.shape[0]
    nseq = tile // rows_per_seq
    shift = rows_per_seq.bit_length() - 1
    rowv = lax.broadcasted_iota(jnp.int32, (tile, D_STATE), 0)
    seq_of_row = lax.shift_right_logical(rowv, shift)
    yoff = []
    for g in range(N_GROUPS):
        cg = cm[:, g * D_STATE:(g + 1) * D_STATE]
        bg = bm[:, g * D_STATE:(g + 1) * D_STATE]
        acc = None
        for b in range(nseq):
            hprev = h_get(b, g)
            if nseq > 1:
                sel = seq_of_row == b
                c_b = jnp.where(sel, cg, 0.0).astype(BF16)
                b_b = jnp.where(sel, bg, 0.0).astype(BF16)
            else:
                c_b, b_b = cg.astype(BF16), bg.astype(BF16)
            term = _mm_nt(c_b, hprev.astype(BF16))
            tick()
            acc = term if acc is None else acc + term
            dec = jnp.exp(tot[b * rows_per_seq:b * rows_per_seq + 1, :])
            scaled = [hprev[e * HEAD_DIM:(e + 1) * HEAD_DIM, :]
                      * dec[:, g * HEADS_PER_GROUP + e:g * HEADS_PER_GROUP + e + 1]
                      for e in range(HEADS_PER_GROUP)]
            upd = _mm(xw_t[g * GROUP_W:(g + 1) * GROUP_W, :], b_b)
            tick()
            h_set(b, g, jnp.concatenate(scaled, axis=0) + upd)
        yoff.append(acc)

    y = part["base"] + jnp.concatenate(yoff, axis=1) * part["seg_e"]
    yz = y * (z * _sigmoid(z))
    outs = []
    for g in range(N_GROUPS):
        blk = yz[:, g * GROUP_W:(g + 1) * GROUP_W]
        ms = jnp.mean(blk * blk, axis=-1, keepdims=True)
        outs.append(blk * lax.rsqrt(ms + EPS) * snorm_g[:, g * GROUP_W:(g + 1) * GROUP_W])
    return jnp.concatenate(outs, axis=1)


def _dt_and_a(dt_raw, dtb, alog, pad_rows, valid_rows_per_tile=None):
    v = dt_raw + dtb
    dt = jnp.maximum(v, 0.0) + jnp.log1p(jnp.exp(-jnp.abs(v)))
    lane = lax.broadcasted_iota(jnp.int32, dt.shape, 1)
    keep = lane < N_HEADS
    row = lax.broadcasted_iota(jnp.int32, dt.shape, 0)
    if pad_rows:
        keep = jnp.logical_and(row >= pad_rows, keep)
    if valid_rows_per_tile is not None:
        keep = jnp.logical_and((row & (SAMPLE_ROWS - 1)) < valid_rows_per_tile, keep)
    dt = jnp.where(keep, dt, 0.0)
    return dt, dt * (-jnp.exp(alog))


def _layernorm_silu(cv, g, b):
    mu = jnp.mean(cv, axis=-1, keepdims=True)
    d = cv - mu
    var = jnp.mean(d * d, axis=-1, keepdims=True)
    cn = d * lax.rsqrt(var + EPS) * g + b
    return cn * _sigmoid(cn)


CONV_ROWS = 32
XCONV_ROWS = 64
CONF_PAD = 32
XBC_PAD = 8
PROJ_COLS = 256


def _lane_blocks(width):
    return width // LANES


def _mixer_p_body(pad_rows, nchunk, *refs):
    s = pl.program_id(0)
    nshared = 4
    sets = refs[-nshared - 8:-nshared]

    @pl.when(s == 0)
    def _():
        for ref in sets:
            ref[...] = jnp.zeros(ref.shape, F32)

    for slot_p in range(2):
        @pl.when(lax.rem(s, 2) == slot_p)
        def _(slot_p=slot_p):
            _mixer_p_step(slot_p, pad_rows, nchunk, *refs)


def _mixer_p_step(slot_p, pad_rows, nchunk, flag_ref, xp_ref, xm_ref, gm_ref, win_ref, wdt_ref, conf0_ref, xbc0_ref,
                  ssm0_ref, cw_ref, cb_ref, lng_ref, lnb_ref, sw_ref, sb_ref, dtb_ref, alog_ref,
                  dexp_ref, sng_ref, e2_ref, wout_ref, h1_ref, conf_out_ref, xbc_out_ref,
                  ssm_out_ref, gbuf0, xbuf0, zbuf0, dtbuf0, gbuf1, xbuf1, zbuf1, dtbuf1,
                  hst, cbuf, xact, cy):
    s = pl.program_id(0)
    m = jnp.maximum(s - 1, 0)
    c = lax.rem(m, nchunk)
    last = nchunk - 1
    tile = xp_ref.shape[0]
    conf_lo = CONF_PAD - (CONF_K - 1)
    xbc_lo = XBC_PAD - (SSM_K - 1)
    sets = ((gbuf0, xbuf0, zbuf0, dtbuf0), (gbuf1, xbuf1, zbuf1, dtbuf1))
    gnext, xnext, znext, dtnext = sets[slot_p]
    gbuf, xbuf, zbuf, dtbuf = sets[1 - slot_p]

    @pl.when(c == 0)
    def _():
        for j in range(_lane_blocks(D_CONV)):
            gbuf[j, conf_lo:CONF_PAD, :] = conf0_ref[0, :, j * LANES:(j + 1) * LANES]
        for j in range(_lane_blocks(XBC)):
            xbuf[j, xbc_lo:XBC_PAD, :] = xbc0_ref[0, :, j * LANES:(j + 1) * LANES]

    u = _rms(xp_ref[...], gm_ref[...]).astype(BF16)
    per = PROJ_COLS // LANES

    def proj_glu(i):
        lin = _mm(u, win_ref[:, i * PROJ_COLS:(i + 1) * PROJ_COLS])
        gate = _mm(u, win_ref[:, OFF_GATE + i * PROJ_COLS:OFF_GATE + (i + 1) * PROJ_COLS])
        glu = lin * _sigmoid(gate)
        for jj in range(per):
            gnext[i * per + jj, CONF_PAD:CONF_PAD + tile, :] = glu[:, jj * LANES:(jj + 1) * LANES]

    def proj_z(i):
        znext[:, i * PROJ_COLS:(i + 1) * PROJ_COLS] = _mm(
            u, win_ref[:, OFF_Z + i * PROJ_COLS:OFF_Z + (i + 1) * PROJ_COLS])

    def proj_xbc(i):
        res = _mm(u, win_ref[:, OFF_XBC + i * PROJ_COLS:OFF_XBC + (i + 1) * PROJ_COLS])
        for jj in range(per):
            xnext[i * per + jj, XBC_PAD:XBC_PAD + tile, :] = res[:, jj * LANES:(jj + 1) * LANES]

    def proj_dt():
        dtnext[...] = _mm(u, wdt_ref[...])

    stage1 = ([(2, functools.partial(proj_glu, i)) for i in range(D_CONV // PROJ_COLS)]
              + [(1, functools.partial(proj_z, i)) for i in range(D_SSM // PROJ_COLS)]
              + [(1, functools.partial(proj_xbc, i)) for i in range(XBC // PROJ_COLS)]
              + [(1, proj_dt)])

    x = xm_ref[...]
    dt_raw = dtbuf[...]

    chain = []

    def conv_start(bias, rows):
        init = jnp.broadcast_to(bias, (rows, LANES))
        if chain:
            never = jnp.broadcast_to(flag_ref[0], (rows, LANES)) != 0
            init = jnp.where(never, chain[-1][0:rows, :], init)
        return init

    def conv_conf(rb, j):
        cols = slice(j * LANES, (j + 1) * LANES)
        acc = conv_start(cb_ref[:, cols], CONV_ROWS)
        for k in range(CONF_K):
            r0 = rb * CONV_ROWS + conf_lo + k
            acc = acc + cw_ref[k:k + 1, cols] * gbuf[j, r0:r0 + CONV_ROWS, :]
        cbuf[rb * CONV_ROWS:(rb + 1) * CONV_ROWS, cols] = acc
        chain.append(acc)

    def conv_xbc(rb, j):
        cols = slice(j * LANES, (j + 1) * LANES)
        acc = conv_start(sb_ref[:, cols], XCONV_ROWS)
        for k in range(SSM_K):
            r0 = rb * XCONV_ROWS + xbc_lo + k
            acc = acc + sw_ref[k:k + 1, cols] * xbuf[j, r0:r0 + XCONV_ROWS, :]
        xact[rb * XCONV_ROWS:(rb + 1) * XCONV_ROWS, cols] = acc * _sigmoid(acc)
        chain.append(acc)

    for _, item in stage1:
        item()
    assert XCONV_ROWS >= CONV_ROWS
    for rb in range(tile // XCONV_ROWS):
        for j in range(_lane_blocks(XBC)):
            conv_xbc(rb, j)
    for rb in range(tile // CONV_ROWS):
        for j in range(_lane_blocks(D_CONV)):
            conv_conf(rb, j)

    @pl.when(c == 0)
    def _():
        hst[...] = ssm0_ref[0].reshape(N_HEADS * HEAD_DIM, D_STATE)

    cy[:, 0:D_CONV] = _layernorm_silu(cbuf[...], lng_ref[...], lnb_ref[...]).astype(BF16)
    gnext[:, 0:CONF_PAD, :] = gbuf[:, tile:tile + CONF_PAD, :]
    xnext[:, 0:XBC_PAD, :] = xbuf[:, tile:tile + XBC_PAD, :]

    dt, a = _dt_and_a(dt_raw, dtb_ref[...], alog_ref[...], pad_rows)

    def h_get(b, g):
        return hst[g * GROUP_W:(g + 1) * GROUP_W, :]

    def h_set(b, g, val):
        hst[g * GROUP_W:(g + 1) * GROUP_W, :] = val

    for j in range(tile // SSD_CHUNK):
        rs = slice(j * SSD_CHUNK, (j + 1) * SSD_CHUNK)
        part = _ssd_intra(xact[rs, 0:D_SSM], xact[rs, D_SSM:D_SSM + GN], xact[rs, D_SSM + GN:XBC],
                          dt[rs], a[rs], SSD_CHUNK, e2_ref[...], dexp_ref[...])
        out = _ssd_inter(part, zbuf[rs, :], SSD_CHUNK, sng_ref[...], h_get, h_set)
        cy[rs, D_CONV:2 * D_CONV] = out.astype(BF16)
        h1_ref[rs, :] = x[rs, :] + _mm(cy[rs, :], wout_ref[...])

    @pl.when(c == last)
    def _():
        for j in range(_lane_blocks(D_CONV)):
            conf_out_ref[0, :, j * LANES:(j + 1) * LANES] = gbuf[j, tile + conf_lo:tile + CONF_PAD, :]
        for j in range(_lane_blocks(XBC)):
            xbc_out_ref[0, :, j * LANES:(j + 1) * LANES] = xbuf[j, tile + xbc_lo:tile + XBC_PAD, :]
        ssm_out_ref[0] = hst[...].reshape(N_HEADS, HEAD_DIM, D_STATE)


def _mixer_p(x, conf0, xbc0, ssm0, p, nbatch, tile, pad_rows):
    rows = x.shape[0]
    ntile = rows // tile
    nchunk = ntile // nbatch
    stage1_map = lambda s: (jnp.minimum(s, ntile - 1), 0)
    stage2_map = lambda s: (jnp.maximum(s - 1, 0), 0)
    state_specs = lambda shp: pl.BlockSpec(
        (1,) + shp, lambda s: (jnp.maximum(s - 1, 0) // nchunk,) + (0,) * len(shp))
    small = [p["conf_w"], p["conf_b"], p["ln_g"], p["ln_b"], p["sconv_w"], p["sconv_b"],
             p["dt_bias"], p["a_log"], p["d_exp"], p["snorm_g"], p["e2"]]
    return pl.pallas_call(
        functools.partial(_mixer_p_body, pad_rows, nchunk),
        out_shape=(jax.ShapeDtypeStruct((rows, D_MODEL), F32),
                   jax.ShapeDtypeStruct((nbatch, CONF_K - 1, D_CONV), F32),
                   jax.ShapeDtypeStruct((nbatch, SSM_K - 1, XBC), F32),
                   jax.ShapeDtypeStruct((nbatch, N_HEADS, HEAD_DIM, D_STATE), F32)),
        grid=(ntile + 1,),
        in_specs=[pl.BlockSpec(memory_space=pltpu.SMEM),
                  pl.BlockSpec((tile, D_MODEL), stage1_map),
                  pl.BlockSpec((tile, D_MODEL), stage2_map),
                  _const_spec((1, D_MODEL)),
                  _const_spec((D_MODEL, OFF_DT)),
                  _const_spec((D_MODEL, LANES)),
                  _const_spec((1, CONF_K - 1, D_CONV)),
                  _const_spec((1, SSM_K - 1, XBC)),
                  _const_spec((1, N_HEADS, HEAD_DIM, D_STATE))]
                 + [_const_spec(s.shape) for s in small]
                 + [_const_spec((2 * D_CONV, D_MODEL))],
        out_specs=(pl.BlockSpec((tile, D_MODEL), stage2_map),
                   state_specs((CONF_K - 1, D_CONV)),
                   state_specs((SSM_K - 1, XBC)),
                   state_specs((N_HEADS, HEAD_DIM, D_STATE))),
        scratch_shapes=2 * [pltpu.VMEM((_lane_blocks(D_CONV), tile + CONF_PAD, LANES), F32),
                            pltpu.VMEM((_lane_blocks(XBC), tile + XBC_PAD, LANES), F32),
                            pltpu.VMEM((tile, D_SSM), F32),
                            pltpu.VMEM((tile, LANES), F32)]
                       + [pltpu.VMEM((N_HEADS * HEAD_DIM, D_STATE), F32),
                        pltpu.VMEM((tile, D_CONV), F32),
                        pltpu.VMEM((tile, XBC), F32),
                        pltpu.VMEM((tile, 2 * D_CONV), BF16)],
        compiler_params=_params(48),
        name="mixer_prompt",
    )(jnp.zeros((1,), jnp.int32), x, x, p["g_mix"], p["w_in"], p["w_dt"], conf0, xbc0, ssm0,
      *small, p["w_out"])


def _mix_s_body(proj_ref, conf_ref, xbc_ref, ssm_ref, cw_ref, cb_ref, lng_ref, lnb_ref, sw_ref,
                sb_ref, dtb_ref, alog_ref, dexp_ref, sng_ref, e2_ref,
                cy_ref, conf_out_ref, xbc_out_ref, ssm_out_ref, stage, tbuf, cbuf, xact):
    nseq = ssm_ref.shape[0]
    rows = nseq * SAMPLE_ROWS
    new_rows = SAMPLE_ROWS // 2

    def rows_at(t):
        return pl.ds(t, nseq, stride=SAMPLE_ROWS)

    def conv_block(value, state_ref, state_out_ref, w_ref, b_ref, cols, out_ref, silu):
        nstate = state_ref.shape[0]
        stage[...] = value
        for i in range(nstate):
            tbuf[i] = state_ref[i, :, cols]
        for t in range(new_rows):
            tbuf[nstate + t] = stage[rows_at(t), :]
        stage[...] = jnp.zeros((rows, LANES), F32)
        for t in range(new_rows):
            acc = jnp.broadcast_to(b_ref[:, cols], (nseq, LANES))
            for k in range(nstate + 1):
                acc = acc + w_ref[k:k + 1, cols] * tbuf[t + k]
            stage[rows_at(t), :] = acc * _sigmoid(acc) if silu else acc
        out_ref[:, cols] = stage[...]
        for i in range(nstate):
            state_out_ref[i, :, cols] = tbuf[i + new_rows]

    for j in range(_lane_blocks(D_CONV)):
        cols = slice(j * LANES, (j + 1) * LANES)
        gate = proj_ref[:, OFF_GATE + j * LANES:OFF_GATE + (j + 1) * LANES]
        conv_block(proj_ref[:, cols] * _sigmoid(gate), conf_ref, conf_out_ref, cw_ref, cb_ref,
                   cols, cbuf, False)
    cy_ref[:, 0:D_CONV] = _layernorm_silu(cbuf[...], lng_ref[...], lnb_ref[...]).astype(BF16)

    for j in range(_lane_blocks(XBC)):
        cols = slice(j * LANES, (j + 1) * LANES)
        conv_block(proj_ref[:, OFF_XBC + j * LANES:OFF_XBC + (j + 1) * LANES], xbc_ref,
                   xbc_out_ref, sw_ref, sb_ref, cols, xact, True)
    xv = xact[...]

    dt, a = _dt_and_a(proj_ref[:, OFF_DT:PROJ_W], dtb_ref[...], alog_ref[...], 0, new_rows)

    def h_get(b, g):
        return ssm_ref[b, g * HEADS_PER_GROUP:(g + 1) * HEADS_PER_GROUP].reshape(GROUP_W, D_STATE)

    def h_set(b, g, val):
        ssm_out_ref[b, g * HEADS_PER_GROUP:(g + 1) * HEADS_PER_GROUP] = val.reshape(
            HEADS_PER_GROUP, HEAD_DIM, D_STATE)

    part = _ssd_intra(xv[:, 0:D_SSM], xv[:, D_SSM:D_SSM + GN], xv[:, D_SSM + GN:XBC], dt, a,
                      SAMPLE_ROWS, e2_ref[...], dexp_ref[...])
    out = _ssd_inter(part, proj_ref[:, OFF_Z:OFF_XBC], SAMPLE_ROWS, sng_ref[...], h_get, h_set)
    cy_ref[:, D_CONV:2 * D_CONV] = out.astype(BF16)


def _mix_s(proj, conf, xbc, ssm, p):
    nbatch = ssm.shape[0]
    rows = SAMPLE_SEQS * SAMPLE_ROWS
    small = [p["conf_w"], p["conf_b"], p["ln_g"], p["ln_b"], p["sconv_w"], p["sconv_b"],
             p["dt_bias"], p["a_log"], p["d_exp"], p["snorm_g"], p["e2"]]
    blk3 = lambda shp: pl.BlockSpec((SAMPLE_SEQS,) + shp, lambda i: (i,) + (0,) * len(shp))
    tmaj = lambda nrow, width: pl.BlockSpec((nrow, SAMPLE_SEQS, width), lambda i: (0, i, 0))
    return pl.pallas_call(
        _mix_s_body,
        out_shape=(jax.ShapeDtypeStruct((nbatch * SAMPLE_ROWS, 2 * D_CONV), BF16),
                   jax.ShapeDtypeStruct(conf.shape, F32),
                   jax.ShapeDtypeStruct(xbc.shape, F32),
                   jax.ShapeDtypeStruct(ssm.shape, F32)),
        grid=(nbatch // SAMPLE_SEQS,),
        in_specs=[pl.BlockSpec((rows, PROJ_W), lambda i: (i, 0)),
                  tmaj(CONF_K - 1, D_CONV), tmaj(SSM_K - 1, XBC),
                  blk3((N_HEADS, HEAD_DIM, D_STATE))]
                 + [_const_spec(s.shape) for s in small],
        out_specs=(pl.BlockSpec((rows, 2 * D_CONV), lambda i: (i, 0)),
                   tmaj(CONF_K - 1, D_CONV), tmaj(SSM_K - 1, XBC),
                   blk3((N_HEADS, HEAD_DIM, D_STATE))),
        scratch_shapes=[pltpu.VMEM((rows, LANES), F32),
                        pltpu.VMEM((CONF_K - 1 + SAMPLE_ROWS // 2, SAMPLE_SEQS, LANES), F32),
                        pltpu.VMEM((rows, D_CONV), F32),
                        pltpu.VMEM((rows, XBC), F32)],
        compiler_params=_params(40),
        name="mix_sample",
    )(proj, conf, xbc, ssm, *small)


FFN_COLS = 256
FFN_PAD = 8


def _ffn_p_body(nchunk, *refs):
    s = pl.program_id(0)
    bufs = refs[-3:-1]

    @pl.when(s == 0)
    def _():
        for ref in bufs:
            ref[...] = jnp.zeros(ref.shape, F32)

    for slot_p in range(2):
        @pl.when(lax.rem(s, 2) == slot_p)
        def _(slot_p=slot_p):
            _ffn_p_step(slot_p, nchunk, *refs)


def _ffn_p_step(slot_p, nchunk, flag_ref, hp_ref, hm_ref, gn_ref, wu_ref, tail0_ref, cw_ref, cb_ref,
                wd_ref, gf_ref, y_ref, tail_out_ref, ubuf0, ubuf1, act):
    s = pl.program_id(0)
    c = lax.rem(jnp.maximum(s - 1, 0), nchunk)
    tile = hp_ref.shape[0]
    lo = FFN_PAD - (FFN_K - 1)
    nblk = _lane_blocks(2 * D_FF)
    half = _lane_blocks(D_FF)
    unext = (ubuf0, ubuf1)[slot_p]
    ubuf = (ubuf0, ubuf1)[1 - slot_p]

    @pl.when(c == 0)
    def _():
        for j in range(nblk):
            ubuf[j, lo:FFN_PAD, :] = tail0_ref[0, :, j * LANES:(j + 1) * LANES]

    u = _rms(hp_ref[...], gn_ref[...]).astype(BF16)
    per = PROJ_COLS // LANES
    for i in range(2 * D_FF // PROJ_COLS):
        res = _mm(u, wu_ref[:, i * PROJ_COLS:(i + 1) * PROJ_COLS])
        for jj in range(per):
            unext[i * per + jj, FFN_PAD:FFN_PAD + tile, :] = res[:, jj * LANES:(jj + 1) * LANES]

    def conv(j, r0, init):
        cols = slice(j * LANES, (j + 1) * LANES)
        out = init
        for k in range(FFN_K):
            out = out + cw_ref[k:k + 1, cols] * ubuf[j, r0 + lo + k:r0 + lo + k + XCONV_ROWS, :]
        return out

    def bias(j):
        return jnp.broadcast_to(cb_ref[:, j * LANES:(j + 1) * LANES], (XCONV_ROWS, LANES))

    prev = None
    for rb in range(tile // XCONV_ROWS):
        r0 = rb * XCONV_ROWS
        for j in range(half):
            init = bias(j)
            if prev is not None:
                never = jnp.broadcast_to(flag_ref[0], (XCONV_ROWS, LANES)) != 0
                init = jnp.where(never, prev, init)
            gate = conv(j, r0, init)
            prev = gate * _sigmoid(gate) * conv(half + j, r0, bias(half + j))
            act[r0:r0 + XCONV_ROWS, j * LANES:(j + 1) * LANES] = prev.astype(BF16)

    @pl.when(c == nchunk - 1)
    def _():
        for j in range(nblk):
            tail_out_ref[0, :, j * LANES:(j + 1) * LANES] = ubuf[j, tile + lo:tile + FFN_PAD, :]

    unext[:, 0:FFN_PAD, :] = ubuf[:, tile:tile + FFN_PAD, :]
    y_ref[...] = _rms(hm_ref[...] + _mm(act[...], wd_ref[...]), gf_ref[...])


def _ffn_p(h1, tail0, p, nbatch, tile):
    rows = h1.shape[0]
    ntile = rows // tile
    nchunk = ntile // nbatch
    stage1_map = lambda s: (jnp.minimum(s, ntile - 1), 0)
    stage2_map = lambda s: (jnp.maximum(s - 1, 0), 0)
    return pl.pallas_call(
        functools.partial(_ffn_p_body, nchunk),
        out_shape=(jax.ShapeDtypeStruct((rows, D_MODEL), F32),
                   jax.ShapeDtypeStruct((nbatch, FFN_K - 1, 2 * D_FF), F32)),
        grid=(ntile + 1,),
        in_specs=[pl.BlockSpec(memory_space=pltpu.SMEM),
                  pl.BlockSpec((tile, D_MODEL), stage1_map),
                  pl.BlockSpec((tile, D_MODEL), stage2_map),
                  _const_spec((1, D_MODEL)),
                  _const_spec((D_MODEL, 2 * D_FF)),
                  _const_spec((1, FFN_K - 1, 2 * D_FF)),
                  _const_spec((FFN_K, 2 * D_FF)),
                  _const_spec((1, 2 * D_FF)),
                  _const_spec((D_FF, D_MODEL)),
                  _const_spec((1, D_MODEL))],
        out_specs=(pl.BlockSpec((tile, D_MODEL), stage2_map),
                   pl.BlockSpec((1, FFN_K - 1, 2 * D_FF),
                                lambda s: (jnp.maximum(s - 1, 0) // nchunk, 0, 0))),
        scratch_shapes=2 * [pltpu.VMEM((_lane_blocks(2 * D_FF), tile + FFN_PAD, LANES), F32)]
                       + [pltpu.VMEM((tile, D_FF), BF16)],
        compiler_params=_params(48),
        name="ffn_prompt",
    )(jnp.zeros((1,), jnp.int32), h1, h1, p["g_ffn"], p["w_up"], tail0, p["fconv_w"],
      p["fconv_b"], p["w_down"], p["g_final"])


def _ffn_s_body(up_ref, h1_ref, st_ref, cw_ref, cb_ref, wd_ref, gf_ref,
                y_ref, st_out_ref, ubuf, act):
    nseq = st_ref.shape[0]
    rows = nseq * SAMPLE_ROWS
    new_rows = SAMPLE_ROWS // 2
    lo = FFN_PAD - (FFN_K - 1)
    ubuf[:, lo:FFN_PAD, :] = st_ref[...]
    ubuf[:, FFN_PAD:FFN_PAD + SAMPLE_ROWS, :] = up_ref[...].reshape(nseq, SAMPLE_ROWS, 2 * D_FF)

    def conv(cols):
        out = jnp.broadcast_to(cb_ref[:, cols].reshape(1, 1, FFN_COLS),
                               (nseq, SAMPLE_ROWS, FFN_COLS))
        for k in range(FFN_K):
            w = cw_ref[k:k + 1, cols].reshape(1, 1, FFN_COLS)
            out = out + w * ubuf[:, lo + k:lo + k + SAMPLE_ROWS, cols]
        return out.reshape(rows, FFN_COLS)

    for j in range(D_FF // FFN_COLS):
        gate = conv(slice(j * FFN_COLS, (j + 1) * FFN_COLS))
        lin = conv(slice(D_FF + j * FFN_COLS, D_FF + (j + 1) * FFN_COLS))
        act[:, j * FFN_COLS:(j + 1) * FFN_COLS] = (gate * _sigmoid(gate) * lin).astype(BF16)

    st_out_ref[...] = ubuf[:, lo + new_rows:FFN_PAD + new_rows, :]
    y_ref[...] = _rms(h1_ref[...] + _mm(act[...], wd_ref[...]), gf_ref[...])


def _ffn_s(up, h1, st, p):
    nbatch = st.shape[0]
    rows = SAMPLE_SEQS * SAMPLE_ROWS
    return pl.pallas_call(
        _ffn_s_body,
        out_shape=(jax.ShapeDtypeStruct((nbatch * SAMPLE_ROWS, D_MODEL), F32),
                   jax.ShapeDtypeStruct(st.shape, F32)),
        grid=(nbatch // SAMPLE_SEQS,),
        in_specs=[pl.BlockSpec((rows, 2 * D_FF), lambda i: (i, 0)),
                  pl.BlockSpec((rows, D_MODEL), lambda i: (i, 0)),
                  pl.BlockSpec((SAMPLE_SEQS, FFN_K - 1, 2 * D_FF), lambda i: (i, 0, 0)),
                  _const_spec((FFN_K, 2 * D_FF)),
                  _const_spec((1, 2 * D_FF)),
                  _const_spec((D_FF, D_MODEL)),
                  _const_spec((1, D_MODEL))],
        out_specs=(pl.BlockSpec((rows, D_MODEL), lambda i: (i, 0)),
                   pl.BlockSpec((SAMPLE_SEQS, FFN_K - 1, 2 * D_FF), lambda i: (i, 0, 0))),
        scratch_shapes=[pltpu.VMEM((SAMPLE_SEQS, FFN_PAD + SAMPLE_ROWS, 2 * D_FF), F32),
                        pltpu.VMEM((rows, D_FF), BF16)],
        compiler_params=_params(40),
        name="ffn_sample",
    )(up, h1, st, p["fconv_w"], p["fconv_b"], p["w_down"], p["g_final"])


def _head_expander():
    k = lax.broadcasted_iota(jnp.int32, (2 * LANES, D_SSM), 0) % LANES
    j = lax.broadcasted_iota(jnp.int32, (2 * LANES, D_SSM), 1) // HEAD_DIM
    return (k == j).astype(BF16)


def _lane_pad(v):
    return jnp.pad(v.astype(F32), (0, LANES - v.shape[0])).reshape(1, LANES)


def kernel(x_prompt, x_sample, state_conf_conv, state_xbc_conv, state_ssm, state_ffn_conv, meta_tokens, norm_mix_g, w_in, conf_conv_w, conf_conv_b, conf_ln_g, conf_ln_b, ssm_conv_w, ssm_conv_b, dt_bias, a_log, d_skip, ssm_norm_g, w_out, norm_ffn_g, w_up, ffn_conv_w, ffn_conv_b, w_down, norm_final_g):
    depth = w_in.shape[0]
    assert depth == 1, "one layer per step"
    nb, seq, _ = x_prompt.shape
    nsb, sseq, _ = x_sample.shape
    assert 2 * sseq == SAMPLE_ROWS and nsb % SAMPLE_SEQS == 0 and N_META <= SSD_CHUNK
    row = lambda v: v.astype(F32).reshape(1, -1)
    p = dict(
        conf_w=conf_conv_w[0], conf_b=row(conf_conv_b[0]), ln_g=row(conf_ln_g[0]),
        ln_b=row(conf_ln_b[0]), sconv_w=ssm_conv_w[0], sconv_b=row(ssm_conv_b[0]),
        dt_bias=_lane_pad(dt_bias[0]), a_log=_lane_pad(a_log[0]),
        d_exp=row(jnp.repeat(d_skip[0], HEAD_DIM)), snorm_g=row(ssm_norm_g[0]),
        e2=_head_expander(), fconv_w=ffn_conv_w[0], fconv_b=row(ffn_conv_b[0]),
        w_down=w_down[0].astype(BF16), g_final=row(norm_final_g),
        g_mix=row(norm_mix_g[0]), g_ffn=row(norm_ffn_g[0]),
        w_in=w_in[0, :, 0:OFF_DT].astype(BF16),
        w_dt=jnp.pad(w_in[0, :, OFF_DT:], ((0, 0), (0, PROJ_W - w_in.shape[2]))).astype(BF16),
        w_out=w_out[0].astype(BF16), w_up=w_up[0].astype(BF16))
    g_mix, g_ffn, w_out_b, w_up_b = p["g_mix"], p["g_ffn"], p["w_out"], p["w_up"]

    meta = jnp.concatenate([jnp.zeros((SSD_CHUNK - N_META, D_MODEL), F32),
                            meta_tokens.astype(F32)], axis=0)
    h1_m, conf0, xbc0, ssm0 = _mixer_p(
        meta, jnp.zeros((1, CONF_K - 1, D_CONV), F32), jnp.zeros((1, SSM_K - 1, XBC), F32),
        jnp.zeros((1, N_HEADS, HEAD_DIM, D_STATE), F32), p, 1, SSD_CHUNK, SSD_CHUNK - N_META)
    _, ffn0 = _ffn_p(h1_m, jnp.zeros((1, FFN_K - 1, 2 * D_FF), F32), p, 1, SSD_CHUNK)

    tile = 256
    xp = x_prompt.reshape(nb * seq, D_MODEL)
    h1_p, p_conf, p_xbc, p_ssm = _mixer_p(xp, conf0, xbc0, ssm0, p, nb, tile, 0)
    y_p, p_ffn = _ffn_p(h1_p, ffn0, p, nb, tile)

    xs = jnp.pad(x_sample, ((0, 0), (0, SAMPLE_ROWS - sseq), (0, 0))).reshape(
        nsb * SAMPLE_ROWS, D_MODEL)
    proj_s = _in_proj(xs, g_mix, p["w_in"], p["w_dt"], tile)
    time_major = lambda v: jnp.transpose(v, (1, 0, 2))
    cy_s, s_conf, s_xbc, s_ssm = _mix_s(proj_s, time_major(state_conf_conv[0]),
                                        time_major(state_xbc_conv[0]), state_ssm[0], p)
    s_conf, s_xbc = time_major(s_conf), time_major(s_xbc)
    h1_s, up_s = _out_up(cy_s, xs, w_out_b, g_ffn, w_up_b, tile)
    y_s, s_ffn = _ffn_s(up_s, h1_s, state_ffn_conv[0], p)

    y_prompt = y_p.reshape(nb, seq, D_MODEL)
    y_sample = y_s.reshape(nsb, SAMPLE_ROWS, D_MODEL)[:, :sseq]
    return (y_prompt, y_sample, p_conf[None], p_xbc[None], p_ssm[None], p_ffn[None],
            s_conf[None], s_xbc[None], s_ssm[None], s_ffn[None])
```

```python
import functools

import jax
import jax.numpy as jnp
from jax import lax
from jax.experimental import pallas as pl
from jax.experimental.pallas import tpu as pltpu

F32 = jnp.float32
BF16 = jnp.bfloat16

D_MODEL = 1024
N_META = 16
D_CONV = 1024
D_SSM = 1024
HEAD_DIM = 64
N_HEADS = D_SSM // HEAD_DIM
N_GROUPS = 2
HEADS_PER_GROUP = N_HEADS // N_GROUPS
D_STATE = 128
GN = N_GROUPS * D_STATE
GROUP_W = D_SSM // N_GROUPS
XBC = D_SSM + 2 * GN
D_FF = 2816
CONF_K = 31
SSM_K = 4
FFN_K = 3
OFF_GATE = D_CONV
OFF_Z = 2 * D_CONV
OFF_XBC = OFF_Z + D_SSM
OFF_DT = OFF_XBC + XBC
LANES = 128
SUBLANES = 8
PROJ_W = OFF_DT + LANES
EPS = 1e-5

SSD_CHUNK = 128
SAMPLE_ROWS = 8
SAMPLE_SEQS = 8
FFN_SAMPLE_SEQS = 16
CAST_COLS = 512
MIB = 1024 * 1024


def _sigmoid(x):
    return 1.0 / (1.0 + jnp.exp(-x))


def _rms(x, g):
    return x * lax.rsqrt(jnp.mean(x * x, axis=-1, keepdims=True) + EPS) * g


def _mm(a, b):
    return jnp.dot(a, b, preferred_element_type=F32)


def _mm_nt(a, b):
    return lax.dot_general(a, b, (((1,), (1,)), ((), ())), preferred_element_type=F32)


def _const_spec(shape):
    zeros = (0,) * len(shape)
    return pl.BlockSpec(shape, lambda *_: zeros, pipeline_mode=pl.Buffered(1))


def _params(vmem_mib, ndims=1, flags=None):
    return pltpu.CompilerParams(dimension_semantics=("arbitrary",) * ndims,
                                vmem_limit_bytes=vmem_mib * MIB, flags=flags)


def _cast_w_in_body(w_ref, wdt_ref, o_ref, odt_ref):
    o_ref[...] = w_ref[...].T.astype(BF16)

    @pl.when(pl.program_id(0) == 0)
    def _():
        ndt, rows = wdt_ref.shape
        padded = jnp.concatenate([wdt_ref[...], jnp.zeros((LANES - ndt, rows), F32)], axis=0)
        odt_ref[...] = padded.T.astype(BF16)


def _cast_w_in(w_in):
    _, rows, width = w_in.shape
    ndt = width - OFF_DT
    assert 0 < ndt <= LANES and ndt % SUBLANES == 0 and OFF_DT % ndt == 0
    w_t = jnp.transpose(w_in[0])
    return pl.pallas_call(
        _cast_w_in_body,
        out_shape=(jax.ShapeDtypeStruct((rows, OFF_DT), BF16),
                   jax.ShapeDtypeStruct((rows, LANES), BF16)),
        grid=(OFF_DT // CAST_COLS,),
        in_specs=[pl.BlockSpec((CAST_COLS, rows), lambda i: (i, 0)),
                  pl.BlockSpec((ndt, rows), lambda i: (OFF_DT // ndt, 0))],
        out_specs=(pl.BlockSpec((rows, CAST_COLS), lambda i: (0, i)),
                   pl.BlockSpec((rows, LANES), lambda i: (0, 0))),
        compiler_params=_params(16),
        name="cast_w_in",
    )(w_t, w_t)


def _in_proj_body(x_ref, g_ref, w_ref, wdt_ref, o_ref):
    u = _rms(x_ref[...], g_ref[...]).astype(BF16)
    o_ref[:, 0:OFF_DT] = _mm(u, w_ref[...])
    o_ref[:, OFF_DT:PROJ_W] = _mm(u, wdt_ref[...])


def _in_proj(x, g, w, w_dt, row_tile):
    rows = x.shape[0]
    return pl.pallas_call(
        _in_proj_body,
        out_shape=jax.ShapeDtypeStruct((rows, PROJ_W), F32),
        grid=(rows // row_tile,),
        in_specs=[pl.BlockSpec((row_tile, D_MODEL), lambda i: (i, 0)),
                  _const_spec((1, D_MODEL)),
                  _const_spec((D_MODEL, OFF_DT)),
                  _const_spec((D_MODEL, LANES))],
        out_specs=pl.BlockSpec((row_tile, PROJ_W), lambda i: (i, 0)),
        compiler_params=_params(44),
        name="in_proj",
    )(x, g, w, w_dt)


def _out_up_body(cy_ref, h_ref, wo_ref, g_ref, wu_ref, h1_ref, up_ref):
    h1 = h_ref[...] + _mm(cy_ref[...], wo_ref[...])
    h1_ref[...] = h1
    up_ref[...] = _mm(_rms(h1, g_ref[...]).astype(BF16), wu_ref[...])


def _out_up(cy, h, w_out, g, w_up, row_tile):
    rows = h.shape[0]
    return pl.pallas_call(
        _out_up_body,
        out_shape=(jax.ShapeDtypeStruct((rows, D_MODEL), F32),
                   jax.ShapeDtypeStruct((rows, 2 * D_FF), F32)),
        grid=(rows // row_tile,),
        in_specs=[pl.BlockSpec((row_tile, 2 * D_CONV), lambda i: (i, 0)),
                  pl.BlockSpec((row_tile, D_MODEL), lambda i: (i, 0)),
                  _const_spec((2 * D_CONV, D_MODEL)),
                  _const_spec((1, D_MODEL)),
                  _const_spec((D_MODEL, 2 * D_FF))],
        out_specs=(pl.BlockSpec((row_tile, D_MODEL), lambda i: (i, 0)),
                   pl.BlockSpec((row_tile, 2 * D_FF), lambda i: (i, 0))),
        compiler_params=_params(52),
        name="out_up",
    )(cy, h, w_out, g, w_up)


def _split3(v):
    v1 = v.astype(BF16)
    r = v - v1.astype(F32)
    v2 = r.astype(BF16)
    v3 = (r - v2.astype(F32)).astype(BF16)
    return v1, v2, v3


def _ssd_intra(xs, bm, cm, dt, a, rows_per_seq, e2, d_exp, tick=lambda: None):
    tile = xs.shape[0]
    nseq = tile // rows_per_seq
    shift = rows_per_seq.bit_length() - 1
    row = lax.broadcasted_iota(jnp.int32, (tile, tile), 0)
    col = lax.broadcasted_iota(jnp.int32, (tile, tile), 1)
    causal = col <= row
    if nseq > 1:
        same = lax.shift_right_logical(row, shift) == lax.shift_right_logical(col, shift)
        mask = jnp.logical_and(same, causal)
        bd = jnp.where(same, 1.0, 0.0).astype(BF16)
    else:
        mask = causal
        bd = jnp.ones((tile, tile), BF16)
    tri = jnp.where(mask, 1.0, 0.0).astype(BF16)

    def mm(p, q, nt=False):
        out = _mm_nt(p, q) if nt else _mm(p, q)
        tick()
        return out

    a1, a2, a3 = _split3(a)
    acs = mm(tri, a1) + mm(tri, a2) + mm(tri, a3)
    tot = mm(bd, a1) + mm(bd, a2) + mm(bd, a3)
    acs_t = acs.T

    def expand(q):
        q1 = q.astype(BF16)
        q2 = (q - q1.astype(F32)).astype(BF16)
        return mm(jnp.concatenate([q1, q2], axis=1), e2)

    xdt = xs * expand(dt)
    xw_t = (xs * expand(dt * jnp.exp(tot - acs))).T.astype(BF16)
    seg_e = expand(jnp.exp(acs))

    lane = lax.broadcasted_iota(jnp.int32, (tile, LANES), 1)
    low_half = lane < HEAD_DIM
    ydiag = []
    for g in range(N_GROUPS):
        cgb = cm[:, g * D_STATE:(g + 1) * D_STATE].astype(BF16)
        bgb = bm[:, g * D_STATE:(g + 1) * D_STATE].astype(BF16)
        cb = mm(cgb, bgb, nt=True)
        for pair in range(HEADS_PER_GROUP // 2):
            blk = g * (HEADS_PER_GROUP // 2) + pair
            xb = xdt[:, blk * LANES:(blk + 1) * LANES]
            acc = None
            for half in range(2):
                h = 2 * blk + half
                diff = acs[:, h:h + 1] - acs_t[h:h + 1, :]
                decay = jnp.where(mask, jnp.exp(jnp.where(mask, diff, 0.0)), 0.0)
                xh = jnp.where(low_half if half == 0 else jnp.logical_not(low_half), xb, 0.0)
                term = mm((cb * decay).astype(BF16), xh.astype(BF16))
                acc = term if acc is None else acc + term
            ydiag.append(acc)
    return dict(base=jnp.concatenate(ydiag, axis=1) + xs * d_exp, xw_t=xw_t, seg_e=seg_e,
                tot=tot, bm=bm, cm=cm)


def _ssd_inter(part, z, rows_per_seq, snorm_g, h_get, h_set, tick=lambda: None):
    bm, cm, tot, xw_t = part["bm"], part["cm"], part["tot"], part["xw_t"]
    tile = bm.shape[0]
    nseq = tile // rows_per_seq
    shift = rows_per_seq.bit_length() - 1
    rowv = lax.broadcasted_iota(jnp.int32, (tile, D_STATE), 0)
    seq_of_row = lax.shift_right_logical(rowv, shift)
    yoff = []
    for g in range(N_GROUPS):
        cg = cm[:, g * D_STATE:(g + 1) * D_STATE]
        bg = bm[:, g * D_STATE:(g + 1) * D_STATE]
        acc = None
        for b in range(nseq):
            hprev = h_get(b, g)
            if nseq > 1:
                sel = seq_of_row == b
                c_b = jnp.where(sel, cg, 0.0).astype(BF16)
                b_b = jnp.where(sel, bg, 0.0).astype(BF16)
            else:
                c_b, b_b = cg.astype(BF16), bg.astype(BF16)
            term = _mm_nt(c_b, hprev.astype(BF16))
            tick()
            acc = term if acc is None else acc + term
            dec = jnp.exp(tot[b * rows_per_seq:b * rows_per_seq + 1, :])
            scaled = [hprev[e * HEAD_DIM:(e + 1) * HEAD_DIM, :]
                      * dec[:, g * HEADS_PER_GROUP + e:g * HEADS_PER_GROUP + e + 1]
                      for e in range(HEADS_PER_GROUP)]
            upd = _mm(xw_t[g * GROUP_W:(g + 1) * GROUP_W, :], b_b)
            tick()
            h_set(b, g, jnp.concatenate(scaled, axis=0) + upd)
        yoff.append(acc)

    y = part["base"] + jnp.concatenate(yoff, axis=1) * part["seg_e"]
    yz = y * (z * _sigmoid(z))
    outs = []
    for g in range(N_GROUPS):
        blk = yz[:, g * GROUP_W:(g + 1) * GROUP_W]
        ms = jnp.mean(blk * blk, axis=-1, keepdims=True)
        outs.append(blk * lax.rsqrt(ms + EPS) * snorm_g[:, g * GROUP_W:(g + 1) * GROUP_W])
    return jnp.concatenate(outs, axis=1)


def _dt_and_a(dt_raw, dtb, alog, pad_rows, valid_rows_per_tile=None):
    v = dt_raw + dtb
    dt = jnp.maximum(v, 0.0) + jnp.log1p(jnp.exp(-jnp.abs(v)))
    lane = lax.broadcasted_iota(jnp.int32, dt.shape, 1)
    keep = lane < N_HEADS
    row = lax.broadcasted_iota(jnp.int32, dt.shape, 0)
    if pad_rows:
        keep = jnp.logical_and(row >= pad_rows, keep)
    if valid_rows_per_tile is not None:
        keep = jnp.logical_and((row & (SAMPLE_ROWS - 1)) < valid_rows_per_tile, keep)
    dt = jnp.where(keep, dt, 0.0)
    return dt, dt * (-jnp.exp(alog))


def _layernorm_silu(cv, g, b):
    mu = jnp.mean(cv, axis=-1, keepdims=True)
    d = cv - mu
    var = jnp.mean(d * d, axis=-1, keepdims=True)
    cn = d * lax.rsqrt(var + EPS) * g + b
    return cn * _sigmoid(cn)


CONV_ROWS = 32
XCONV_ROWS = 64
CONF_PAD = 32
XBC_PAD = 8
PROJ_COLS = 256


def _lane_blocks(width):
    return width // LANES


def _mixer_p_body(pad_rows, nchunk, *refs):
    s = pl.program_id(0)
    nshared = 4
    sets = refs[-nshared - 8:-nshared]

    @pl.when(s == 0)
    def _():
        for ref in sets:
            ref[...] = jnp.zeros(ref.shape, F32)

    for slot_p in range(2):
        @pl.when(lax.rem(s, 2) == slot_p)
        def _(slot_p=slot_p):
            _mixer_p_step(slot_p, pad_rows, nchunk, *refs)


def _mixer_p_step(slot_p, pad_rows, nchunk, flag_ref, xp_ref, xm_ref, gm_ref, win_ref, wdt_ref, conf0_ref, xbc0_ref,
                  ssm0_ref, cw_ref, cb_ref, lng_ref, lnb_ref, sw_ref, sb_ref, dtb_ref, alog_ref,
                  dexp_ref, sng_ref, e2_ref, wout_ref, h1_ref, conf_out_ref, xbc_out_ref,
                  ssm_out_ref, gbuf0, xbuf0, zbuf0, dtbuf0, gbuf1, xbuf1, zbuf1, dtbuf1,
                  hst, cbuf, xact, cy):
    s = pl.program_id(0)
    m = jnp.maximum(s - 1, 0)
    c = lax.rem(m, nchunk)
    last = nchunk - 1
    tile = xp_ref.shape[0]
    conf_lo = CONF_PAD - (CONF_K - 1)
    xbc_lo = XBC_PAD - (SSM_K - 1)
    sets = ((gbuf0, xbuf0, zbuf0, dtbuf0), (gbuf1, xbuf1, zbuf1, dtbuf1))
    gnext, xnext, znext, dtnext = sets[slot_p]
    gbuf, xbuf, zbuf, dtbuf = sets[1 - slot_p]

    @pl.when(c == 0)
    def _():
        for j in range(_lane_blocks(D_CONV)):
            gbuf[j, conf_lo:CONF_PAD, :] = conf0_ref[0, :, j * LANES:(j + 1) * LANES]
        for j in range(_lane_blocks(XBC)):
            xbuf[j, xbc_lo:XBC_PAD, :] = xbc0_ref[0, :, j * LANES:(j + 1) * LANES]

    u = _rms(xp_ref[...], gm_ref[...]).astype(BF16)
    per = PROJ_COLS // LANES

    def proj_glu(i):
        lin = _mm(u, win_ref[:, i * PROJ_COLS:(i + 1) * PROJ_COLS])
        gate = _mm(u, win_ref[:, OFF_GATE + i * PROJ_COLS:OFF_GATE + (i + 1) * PROJ_COLS])
        glu = lin * _sigmoid(gate)
        for jj in range(per):
            gnext[i * per + jj, CONF_PAD:CONF_PAD + tile, :] = glu[:, jj * LANES:(jj + 1) * LANES]

    def proj_z(i):
        znext[:, i * PROJ_COLS:(i + 1) * PROJ_COLS] = _mm(
            u, win_ref[:, OFF_Z + i * PROJ_COLS:OFF_Z + (i + 1) * PROJ_COLS])

    def proj_xbc(i):
        res = _mm(u, win_ref[:, OFF_XBC + i * PROJ_COLS:OFF_XBC + (i + 1) * PROJ_COLS])
        for jj in range(per):
            xnext[i * per + jj, XBC_PAD:XBC_PAD + tile, :] = res[:, jj * LANES:(jj + 1) * LANES]

    def proj_dt():
        dtnext[...] = _mm(u, wdt_ref[...])

    stage1 = ([(2, functools.partial(proj_glu, i)) for i in range(D_CONV // PROJ_COLS)]
              + [(1, functools.partial(proj_z, i)) for i in range(D_SSM // PROJ_COLS)]
              + [(1, functools.partial(proj_xbc, i)) for i in range(XBC // PROJ_COLS)]
              + [(1, proj_dt)])

    x = xm_ref[...]
    dt_raw = dtbuf[...]

    chain = []

    def conv_start(bias, rows):
        init = jnp.broadcast_to(bias, (rows, LANES))
        if chain:
            never = jnp.broadcast_to(flag_ref[0], (rows, LANES)) != 0
            init = jnp.where(never, chain[-1][0:rows, :], init)
        return init

    def conv_conf(rb, j):
        cols = slice(j * LANES, (j + 1) * LANES)
        acc = conv_start(cb_ref[:, cols], CONV_ROWS)
        for k in range(CONF_K):
            r0 = rb * CONV_ROWS + conf_lo + k
            acc = acc + cw_ref[k:k + 1, cols] * gbuf[j, r0:r0 + CONV_ROWS, :]
        cbuf[rb * CONV_ROWS:(rb + 1) * CONV_ROWS, cols] = acc
        chain.append(acc)

    def conv_xbc(rb, j):
        cols = slice(j * LANES, (j + 1) * LANES)
        acc = conv_start(sb_ref[:, cols], XCONV_ROWS)
        for k in range(SSM_K):
            r0 = rb * XCONV_ROWS + xbc_lo + k
            acc = acc + sw_ref[k:k + 1, cols] * xbuf[j, r0:r0 + XCONV_ROWS, :]
        xact[rb * XCONV_ROWS:(rb + 1) * XCONV_ROWS, cols] = acc * _sigmoid(acc)
        chain.append(acc)

    for _, item in stage1:
        item()
    assert XCONV_ROWS >= CONV_ROWS
    for rb in range(tile // XCONV_ROWS):
        for j in range(_lane_blocks(XBC)):
            conv_xbc(rb, j)
    for rb in range(tile // CONV_ROWS):
        for j in range(_lane_blocks(D_CONV)):
            conv_conf(rb, j)

    @pl.when(c == 0)
    def _():
        hst[...] = ssm0_ref[0].reshape(N_HEADS * HEAD_DIM, D_STATE)

    cy[:, 0:D_CONV] = _layernorm_silu(cbuf[...], lng_ref[...], lnb_ref[...]).astype(BF16)
    gnext[:, 0:CONF_PAD, :] = gbuf[:, tile:tile + CONF_PAD, :]
    xnext[:, 0:XBC_PAD, :] = xbuf[:, tile:tile + XBC_PAD, :]

    dt, a = _dt_and_a(dt_raw, dtb_ref[...], alog_ref[...], pad_rows)

    def h_get(b, g):
        return hst[g * GROUP_W:(g + 1) * GROUP_W, :]

    def h_set(b, g, val):
        hst[g * GROUP_W:(g + 1) * GROUP_W, :] = val

    for j in range(tile // SSD_CHUNK):
        rs = slice(j * SSD_CHUNK, (j + 1) * SSD_CHUNK)
        part = _ssd_intra(xact[rs, 0:D_SSM], xact[rs, D_SSM:D_SSM + GN], xact[rs, D_SSM + GN:XBC],
                          dt[rs], a[rs], SSD_CHUNK, e2_ref[...], dexp_ref[...])
        out = _ssd_inter(part, zbuf[rs, :], SSD_CHUNK, sng_ref[...], h_get, h_set)
        cy[rs, D_CONV:2 * D_CONV] = out.astype(BF16)
        h1_ref[rs, :] = x[rs, :] + _mm(cy[rs, :], wout_ref[...])

    @pl.when(c == last)
    def _():
        for j in range(_lane_blocks(D_CONV)):
            conf_out_ref[0, :, j * LANES:(j + 1) * LANES] = gbuf[j, tile + conf_lo:tile + CONF_PAD, :]
        for j in range(_lane_blocks(XBC)):
            xbc_out_ref[0, :, j * LANES:(j + 1) * LANES] = xbuf[j, tile + xbc_lo:tile + XBC_PAD, :]
        ssm_out_ref[0] = hst[...].reshape(N_HEADS, HEAD_DIM, D_STATE)


def _mixer_p(x, conf0, xbc0, ssm0, p, nbatch, tile, pad_rows):
    rows = x.shape[0]
    ntile = rows // tile
    nchunk = ntile // nbatch
    stage1_map = lambda s: (jnp.minimum(s, ntile - 1), 0)
    stage2_map = lambda s: (jnp.maximum(s - 1, 0), 0)
    state_specs = lambda shp: pl.BlockSpec(
        (1,) + shp, lambda s: (jnp.maximum(s - 1, 0) // nchunk,) + (0,) * len(shp))
    small = [p["conf_w"], p["conf_b"], p["ln_g"], p["ln_b"], p["sconv_w"], p["sconv_b"],
             p["dt_bias"], p["a_log"], p["d_exp"], p["snorm_g"], p["e2"]]
    return pl.pallas_call(
        functools.partial(_mixer_p_body, pad_rows, nchunk),
        out_shape=(jax.ShapeDtypeStruct((rows, D_MODEL), F32),
                   jax.ShapeDtypeStruct((nbatch, CONF_K - 1, D_CONV), F32),
                   jax.ShapeDtypeStruct((nbatch, SSM_K - 1, XBC), F32),
                   jax.ShapeDtypeStruct((nbatch, N_HEADS, HEAD_DIM, D_STATE), F32)),
        grid=(ntile + 1,),
        in_specs=[pl.BlockSpec(memory_space=pltpu.SMEM),
                  pl.BlockSpec((tile, D_MODEL), stage1_map),
                  pl.BlockSpec((tile, D_MODEL), stage2_map),
                  _const_spec((1, D_MODEL)),
                  _const_spec((D_MODEL, OFF_DT)),
                  _const_spec((D_MODEL, LANES)),
                  _const_spec((1, CONF_K - 1, D_CONV)),
                  _const_spec((1, SSM_K - 1, XBC)),
                  _const_spec((1, N_HEADS, HEAD_DIM, D_STATE))]
                 + [_const_spec(s.shape) for s in small]
                 + [_const_spec((2 * D_CONV, D_MODEL))],
        out_specs=(pl.BlockSpec((tile, D_MODEL), stage2_map),
                   state_specs((CONF_K - 1, D_CONV)),
                   state_specs((SSM_K - 1, XBC)),
                   state_specs((N_HEADS, HEAD_DIM, D_STATE))),
        scratch_shapes=2 * [pltpu.VMEM((_lane_blocks(D_CONV), tile + CONF_PAD, LANES), F32),
                            pltpu.VMEM((_lane_blocks(XBC), tile + XBC_PAD, LANES), F32),
                            pltpu.VMEM((tile, D_SSM), F32),
                            pltpu.VMEM((tile, LANES), F32)]
                       + [pltpu.VMEM((N_HEADS * HEAD_DIM, D_STATE), F32),
                        pltpu.VMEM((tile, D_CONV), F32),
                        pltpu.VMEM((tile, XBC), F32),
                        pltpu.VMEM((tile, 2 * D_CONV), BF16)],
        compiler_params=_params(48),
        name="mixer_prompt",
    )(jnp.zeros((1,), jnp.int32), x, x, p["g_mix"], p["w_in"], p["w_dt"], conf0, xbc0, ssm0,
      *small, p["w_out"])


def _mix_s_body(proj_ref, conf_ref, xbc_ref, ssm_ref, cw_ref, cb_ref, lng_ref, lnb_ref, sw_ref,
                sb_ref, dtb_ref, alog_ref, dexp_ref, sng_ref, e2_ref,
                cy_ref, conf_out_ref, xbc_out_ref, ssm_out_ref, stage, tbuf, cbuf, xact):
    nseq = ssm_ref.shape[0]
    rows = nseq * SAMPLE_ROWS
    new_rows = SAMPLE_ROWS // 2

    def rows_at(t):
        return pl.ds(t, nseq, stride=SAMPLE_ROWS)

    def conv_block(value, state_ref, state_out_ref, w_ref, b_ref, cols, out_ref, silu):
        nstate = state_ref.shape[0]
        stage[...] = value
        for i in range(nstate):
            tbuf[i] = state_ref[i, :, cols]
        for t in range(new_rows):
            tbuf[nstate + t] = stage[rows_at(t), :]
        stage[...] = jnp.zeros((rows, LANES), F32)
        for t in range(new_rows):
            acc = jnp.broadcast_to(b_ref[:, cols], (nseq, LANES))
            for k in range(nstate + 1):
                acc = acc + w_ref[k:k + 1, cols] * tbuf[t + k]
            stage[rows_at(t), :] = acc * _sigmoid(acc) if silu else acc
        out_ref[:, cols] = stage[...]
        for i in range(nstate):
            state_out_ref[i, :, cols] = tbuf[i + new_rows]

    for j in range(_lane_blocks(D_CONV)):
        cols = slice(j * LANES, (j + 1) * LANES)
        gate = proj_ref[:, OFF_GATE + j * LANES:OFF_GATE + (j + 1) * LANES]
        conv_block(proj_ref[:, cols] * _sigmoid(gate), conf_ref, conf_out_ref, cw_ref, cb_ref,
                   cols, cbuf, False)
    cy_ref[:, 0:D_CONV] = _layernorm_silu(cbuf[...], lng_ref[...], lnb_ref[...]).astype(BF16)

    for j in range(_lane_blocks(XBC)):
        cols = slice(j * LANES, (j + 1) * LANES)
        conv_block(proj_ref[:, OFF_XBC + j * LANES:OFF_XBC + (j + 1) * LANES], xbc_ref,
                   xbc_out_ref, sw_ref, sb_ref, cols, xact, True)
    xv = xact[...]

    dt, a = _dt_and_a(proj_ref[:, OFF_DT:PROJ_W], dtb_ref[...], alog_ref[...], 0, new_rows)

    def h_get(b, g):
        return ssm_ref[b, g * HEADS_PER_GROUP:(g + 1) * HEADS_PER_GROUP].reshape(GROUP_W, D_STATE)

    def h_set(b, g, val):
        ssm_out_ref[b, g * HEADS_PER_GROUP:(g + 1) * HEADS_PER_GROUP] = val.reshape(
            HEADS_PER_GROUP, HEAD_DIM, D_STATE)

    part = _ssd_intra(xv[:, 0:D_SSM], xv[:, D_SSM:D_SSM + GN], xv[:, D_SSM + GN:XBC], dt, a,
                      SAMPLE_ROWS, e2_ref[...], dexp_ref[...])
    out = _ssd_inter(part, proj_ref[:, OFF_Z:OFF_XBC], SAMPLE_ROWS, sng_ref[...], h_get, h_set)
    cy_ref[:, D_CONV:2 * D_CONV] = out.astype(BF16)


def _mix_s(proj, conf, xbc, ssm, p):
    nbatch = ssm.shape[0]
    rows = SAMPLE_SEQS * SAMPLE_ROWS
    small = [p["conf_w"], p["conf_b"], p["ln_g"], p["ln_b"], p["sconv_w"], p["sconv_b"],
             p["dt_bias"], p["a_log"], p["d_exp"], p["snorm_g"], p["e2"]]
    blk3 = lambda shp: pl.BlockSpec((SAMPLE_SEQS,) + shp, lambda i: (i,) + (0,) * len(shp))
    tmaj = lambda nrow, width: pl.BlockSpec((nrow, SAMPLE_SEQS, width), lambda i: (0, i, 0))
    return pl.pallas_call(
        _mix_s_body,
        out_shape=(jax.ShapeDtypeStruct((nbatch * SAMPLE_ROWS, 2 * D_CONV), BF16),
                   jax.ShapeDtypeStruct(conf.shape, F32),
                   jax.ShapeDtypeStruct(xbc.shape, F32),
                   jax.ShapeDtypeStruct(ssm.shape, F32)),
        grid=(nbatch // SAMPLE_SEQS,),
        in_specs=[pl.BlockSpec((rows, PROJ_W), lambda i: (i, 0)),
                  tmaj(CONF_K - 1, D_CONV), tmaj(SSM_K - 1, XBC),
                  blk3((N_HEADS, HEAD_DIM, D_STATE))]
                 + [_const_spec(s.shape) for s in small],
        out_specs=(pl.BlockSpec((rows, 2 * D_CONV), lambda i: (i, 0)),
                   tmaj(CONF_K - 1, D_CONV), tmaj(SSM_K - 1, XBC),
                   blk3((N_HEADS, HEAD_DIM, D_STATE))),
        scratch_shapes=[pltpu.VMEM((rows, LANES), F32),
                        pltpu.VMEM((CONF_K - 1 + SAMPLE_ROWS // 2, SAMPLE_SEQS, LANES), F32),
                        pltpu.VMEM((rows, D_CONV), F32),
                        pltpu.VMEM((rows, XBC), F32)],
        compiler_params=_params(40),
        name="mix_sample",
    )(proj, conf, xbc, ssm, *small)


FFN_COLS = 256
FFN_PAD = 8


def _ffn_p_body(nchunk, *refs):
    s = pl.program_id(0)
    bufs = refs[-3:-1]

    @pl.when(s == 0)
    def _():
        for ref in bufs:
            ref[...] = jnp.zeros(ref.shape, F32)

    for slot_p in range(2):
        @pl.when(lax.rem(s, 2) == slot_p)
        def _(slot_p=slot_p):
            _ffn_p_step(slot_p, nchunk, *refs)


def _ffn_p_step(slot_p, nchunk, flag_ref, hp_ref, hm_ref, gn_ref, wu_ref, tail0_ref, cw_ref, cb_ref,
                wd_ref, gf_ref, y_ref, tail_out_ref, ubuf0, ubuf1, act):
    s = pl.program_id(0)
    c = lax.rem(jnp.maximum(s - 1, 0), nchunk)
    tile = hp_ref.shape[0]
    lo = FFN_PAD - (FFN_K - 1)
    nblk = _lane_blocks(2 * D_FF)
    half = _lane_blocks(D_FF)
    unext = (ubuf0, ubuf1)[slot_p]
    ubuf = (ubuf0, ubuf1)[1 - slot_p]

    @pl.when(c == 0)
    def _():
        for j in range(nblk):
            ubuf[j, lo:FFN_PAD, :] = tail0_ref[0, :, j * LANES:(j + 1) * LANES]

    u = _rms(hp_ref[...], gn_ref[...]).astype(BF16)
    per = PROJ_COLS // LANES
    for i in range(2 * D_FF // PROJ_COLS):
        res = _mm(u, wu_ref[:, i * PROJ_COLS:(i + 1) * PROJ_COLS])
        for jj in range(per):
            unext[i * per + jj, FFN_PAD:FFN_PAD + tile, :] = res[:, jj * LANES:(jj + 1) * LANES]

    def conv(j, r0, init):
        cols = slice(j * LANES, (j + 1) * LANES)
        out = init
        for k in range(FFN_K):
            out = out + cw_ref[k:k + 1, cols] * ubuf[j, r0 + lo + k:r0 + lo + k + XCONV_ROWS, :]
        return out

    def bias(j):
        return jnp.broadcast_to(cb_ref[:, j * LANES:(j + 1) * LANES], (XCONV_ROWS, LANES))

    prev = None
    for rb in range(tile // XCONV_ROWS):
        r0 = rb * XCONV_ROWS
        for j in range(half):
            init = bias(j)
            if prev is not None:
                never = jnp.broadcast_to(flag_ref[0], (XCONV_ROWS, LANES)) != 0
                init = jnp.where(never, prev, init)
            gate = conv(j, r0, init)
            prev = gate * _sigmoid(gate) * conv(half + j, r0, bias(half + j))
            act[r0:r0 + XCONV_ROWS, j * LANES:(j + 1) * LANES] = prev.astype(BF16)

    @pl.when(c == nchunk - 1)
    def _():
        for j in range(nblk):
            tail_out_ref[0, :, j * LANES:(j + 1) * LANES] = ubuf[j, tile + lo:tile + FFN_PAD, :]

    unext[:, 0:FFN_PAD, :] = ubuf[:, tile:tile + FFN_PAD, :]
    y_ref[...] = _rms(hm_ref[...] + _mm(act[...], wd_ref[...]), gf_ref[...])


def _ffn_p(h1, tail0, p, nbatch, tile):
    rows = h1.shape[0]
    ntile = rows // tile
    nchunk = ntile // nbatch
    stage1_map = lambda s: (jnp.minimum(s, ntile - 1), 0)
    stage2_map = lambda s: (jnp.maximum(s - 1, 0), 0)
    return pl.pallas_call(
        functools.partial(_ffn_p_body, nchunk),
        out_shape=(jax.ShapeDtypeStruct((rows, D_MODEL), F32),
                   jax.ShapeDtypeStruct((nbatch, FFN_K - 1, 2 * D_FF), F32)),
        grid=(ntile + 1,),
        in_specs=[pl.BlockSpec(memory_space=pltpu.SMEM),
                  pl.BlockSpec((tile, D_MODEL), stage1_map),
                  pl.BlockSpec((tile, D_MODEL), stage2_map),
                  _const_spec((1, D_MODEL)),
                  _const_spec((D_MODEL, 2 * D_FF)),
                  _const_spec((1, FFN_K - 1, 2 * D_FF)),
                  _const_spec((FFN_K, 2 * D_FF)),
                  _const_spec((1, 2 * D_FF)),
                  _const_spec((D_FF, D_MODEL)),
                  _const_spec((1, D_MODEL))],
        out_specs=(pl.BlockSpec((tile, D_MODEL), stage2_map),
                   pl.BlockSpec((1, FFN_K - 1, 2 * D_FF),
                                lambda s: (jnp.maximum(s - 1, 0) // nchunk, 0, 0))),
        scratch_shapes=2 * [pltpu.VMEM((_lane_blocks(2 * D_FF), tile + FFN_PAD, LANES), F32)]
                       + [pltpu.VMEM((tile, D_FF), BF16)],
        compiler_params=_params(48),
        name="ffn_prompt",
    )(jnp.zeros((1,), jnp.int32), h1, h1, p["g_ffn"], p["w_up"], tail0, p["fconv_w"],
      p["fconv_b"], p["w_down"], p["g_final"])


def _ffn_s_body(up_ref, h1_ref, st_ref, cw_ref, cb_ref, wd_ref, gf_ref,
                y_ref, st_out_ref, ubuf, act):
    nseq = st_ref.shape[0]
    rows = nseq * SAMPLE_ROWS
    new_rows = SAMPLE_ROWS // 2
    lo = FFN_PAD - (FFN_K - 1)
    ubuf[:, lo:FFN_PAD, :] = st_ref[...]
    ubuf[:, FFN_PAD:FFN_PAD + SAMPLE_ROWS, :] = up_ref[...].reshape(nseq, SAMPLE_ROWS, 2 * D_FF)

    def conv(cols):
        out = jnp.broadcast_to(cb_ref[:, cols].reshape(1, 1, FFN_COLS),
                               (nseq, SAMPLE_ROWS, FFN_COLS))
        for k in range(FFN_K):
            w = cw_ref[k:k + 1, cols].reshape(1, 1, FFN_COLS)
            out = out + w * ubuf[:, lo + k:lo + k + SAMPLE_ROWS, cols]
        return out.reshape(rows, FFN_COLS)

    for j in range(D_FF // FFN_COLS):
        gate = conv(slice(j * FFN_COLS, (j + 1) * FFN_COLS))
        lin = conv(slice(D_FF + j * FFN_COLS, D_FF + (j + 1) * FFN_COLS))
        act[:, j * FFN_COLS:(j + 1) * FFN_COLS] = (gate * _sigmoid(gate) * lin).astype(BF16)

    st_out_ref[...] = ubuf[:, lo + new_rows:FFN_PAD + new_rows, :]
    y_ref[...] = _rms(h1_ref[...] + _mm(act[...], wd_ref[...]), gf_ref[...])


def _ffn_s(up, h1, st, p):
    nbatch = st.shape[0]
    nseq = FFN_SAMPLE_SEQS
    assert nbatch % nseq == 0
    rows = nseq * SAMPLE_ROWS
    return pl.pallas_call(
        _ffn_s_body,
        out_shape=(jax.ShapeDtypeStruct((nbatch * SAMPLE_ROWS, D_MODEL), F32),
                   jax.ShapeDtypeStruct(st.shape, F32)),
        grid=(nbatch // nseq,),
        in_specs=[pl.BlockSpec((rows, 2 * D_FF), lambda i: (i, 0)),
                  pl.BlockSpec((rows, D_MODEL), lambda i: (i, 0)),
                  pl.BlockSpec((nseq, FFN_K - 1, 2 * D_FF), lambda i: (i, 0, 0)),
                  _const_spec((FFN_K, 2 * D_FF)),
                  _const_spec((1, 2 * D_FF)),
                  _const_spec((D_FF, D_MODEL)),
                  _const_spec((1, D_MODEL))],
        out_specs=(pl.BlockSpec((rows, D_MODEL), lambda i: (i, 0)),
                   pl.BlockSpec((nseq, FFN_K - 1, 2 * D_FF), lambda i: (i, 0, 0))),
        scratch_shapes=[pltpu.VMEM((nseq, FFN_PAD + SAMPLE_ROWS, 2 * D_FF), F32),
                        pltpu.VMEM((rows, D_FF), BF16)],
        compiler_params=_params(40),
        name="ffn_sample",
    )(up, h1, st, p["fconv_w"], p["fconv_b"], p["w_down"], p["g_final"])


def _head_expander():
    k = lax.broadcasted_iota(jnp.int32, (2 * LANES, D_SSM), 0) % LANES
    j = lax.broadcasted_iota(jnp.int32, (2 * LANES, D_SSM), 1) // HEAD_DIM
    return (k == j).astype(BF16)


def _lane_pad(v):
    return jnp.pad(v.astype(F32), (0, LANES - v.shape[0])).reshape(1, LANES)


def kernel(x_prompt, x_sample, state_conf_conv, state_xbc_conv, state_ssm, state_ffn_conv, meta_tokens, norm_mix_g, w_in, conf_conv_w, conf_conv_b, conf_ln_g, conf_ln_b, ssm_conv_w, ssm_conv_b, dt_bias, a_log, d_skip, ssm_norm_g, w_out, norm_ffn_g, w_up, ffn_conv_w, ffn_conv_b, w_down, norm_final_g):
    depth = w_in.shape[0]
    assert depth == 1, "one layer per step"
    nb, seq, _ = x_prompt.shape
    nsb, sseq, _ = x_sample.shape
    assert 2 * sseq == SAMPLE_ROWS and nsb % SAMPLE_SEQS == 0 and N_META <= SSD_CHUNK
    row = lambda v: v.astype(F32).reshape(1, -1)
    p = dict(
        conf_w=conf_conv_w[0], conf_b=row(conf_conv_b[0]), ln_g=row(conf_ln_g[0]),
        ln_b=row(conf_ln_b[0]), sconv_w=ssm_conv_w[0], sconv_b=row(ssm_conv_b[0]),
        dt_bias=_lane_pad(dt_bias[0]), a_log=_lane_pad(a_log[0]),
        d_exp=row(jnp.repeat(d_skip[0], HEAD_DIM)), snorm_g=row(ssm_norm_g[0]),
        e2=_head_expander(), fconv_w=ffn_conv_w[0], fconv_b=row(ffn_conv_b[0]),
        w_down=w_down[0].astype(BF16), g_final=row(norm_final_g),
        g_mix=row(norm_mix_g[0]), g_ffn=row(norm_ffn_g[0]),
        w_out=w_out[0].astype(BF16), w_up=w_up[0].astype(BF16))
    p["w_in"], p["w_dt"] = _cast_w_in(w_in)
    g_mix, g_ffn, w_out_b, w_up_b = p["g_mix"], p["g_ffn"], p["w_out"], p["w_up"]

    meta = jnp.concatenate([jnp.zeros((SSD_CHUNK - N_META, D_MODEL), F32),
                            meta_tokens.astype(F32)], axis=0)
    h1_m, conf0, xbc0, ssm0 = _mixer_p(
        meta, jnp.zeros((1, CONF_K - 1, D_CONV), F32), jnp.zeros((1, SSM_K - 1, XBC), F32),
        jnp.zeros((1, N_HEADS, HEAD_DIM, D_STATE), F32), p, 1, SSD_CHUNK, SSD_CHUNK - N_META)
    _, ffn0 = _ffn_p(h1_m, jnp.zeros((1, FFN_K - 1, 2 * D_FF), F32), p, 1, SSD_CHUNK)

    tile = 256
    xp = x_prompt.reshape(nb * seq, D_MODEL)
    h1_p, p_conf, p_xbc, p_ssm = _mixer_p(xp, conf0, xbc0, ssm0, p, nb, tile, 0)
    y_p, p_ffn = _ffn_p(h1_p, ffn0, p, nb, tile)

    xs = jnp.pad(x_sample, ((0, 0), (0, SAMPLE_ROWS - sseq), (0, 0))).reshape(
        nsb * SAMPLE_ROWS, D_MODEL)
    proj_s = _in_proj(xs, g_mix, p["w_in"], p["w_dt"], tile)
    time_major = lambda v: jnp.transpose(v, (1, 0, 2))
    cy_s, s_conf, s_xbc, s_ssm = _mix_s(proj_s, time_major(state_conf_conv[0]),
                                        time_major(state_xbc_conv[0]), state_ssm[0], p)
    s_conf, s_xbc = time_major(s_conf), time_major(s_xbc)
    h1_s, up_s = _out_up(cy_s, xs, w_out_b, g_ffn, w_up_b, tile)
    y_s, s_ffn = _ffn_s(up_s, h1_s, state_ffn_conv[0], p)

    y_prompt = y_p.reshape(nb, seq, D_MODEL)
    y_sample = y_s.reshape(nsb, SAMPLE_ROWS, D_MODEL)[:, :sseq]
    return (y_prompt, y_sample, p_conf[None], p_xbc[None], p_ssm[None], p_ffn[None],
            s_conf[None], s_xbc[None], s_ssm[None], s_ffn[None])
```

```python
import functools

import jax
import jax.numpy as jnp
from jax import lax
from jax.experimental import pallas as pl
from jax.experimental.pallas import tpu as pltpu

F32 = jnp.float32
BF16 = jnp.bfloat16

D_MODEL = 1024
N_META = 16
D_CONV = 1024
D_SSM = 1024
HEAD_DIM = 64
N_HEADS = D_SSM // HEAD_DIM
N_GROUPS = 2
HEADS_PER_GROUP = N_HEADS // N_GROUPS
D_STATE = 128
GN = N_GROUPS * D_STATE
GROUP_W = D_SSM // N_GROUPS
XBC = D_SSM + 2 * GN
D_FF = 2816
CONF_K = 31
SSM_K = 4
FFN_K = 3
OFF_GATE = D_CONV
OFF_Z = 2 * D_CONV
OFF_XBC = OFF_Z + D_SSM
OFF_DT = OFF_XBC + XBC
LANES = 128
SUBLANES = 8
PROJ_W = OFF_DT + LANES
EPS = 1e-5

SSD_CHUNK = 128
SAMPLE_ROWS = 8
SAMPLE_SEQS = 8
FFN_SAMPLE_SEQS = 16
CAST_COLS = 512
MIB = 1024 * 1024


def _sigmoid(x):
    return 1.0 / (1.0 + jnp.exp(-x))


def _rms(x, g):
    return x * lax.rsqrt(jnp.mean(x * x, axis=-1, keepdims=True) + EPS) * g


def _mm(a, b):
    return jnp.dot(a, b, preferred_element_type=F32)


def _mm_nt(a, b):
    return lax.dot_general(a, b, (((1,), (1,)), ((), ())), preferred_element_type=F32)


def _const_spec(shape):
    zeros = (0,) * len(shape)
    return pl.BlockSpec(shape, lambda *_: zeros, pipeline_mode=pl.Buffered(1))


def _params(vmem_mib, ndims=1, flags=None):
    return pltpu.CompilerParams(dimension_semantics=("arbitrary",) * ndims,
                                vmem_limit_bytes=vmem_mib * MIB, flags=flags)


def _cast_w_in_body(w_ref, wdt_ref, o_ref, odt_ref):
    o_ref[...] = w_ref[...].T.astype(BF16)

    @pl.when(pl.program_id(0) == 0)
    def _():
        ndt, rows = wdt_ref.shape
        padded = jnp.concatenate([wdt_ref[...], jnp.zeros((LANES - ndt, rows), F32)], axis=0)
        odt_ref[...] = padded.T.astype(BF16)


def _cast_w_in(w_in):
    _, rows, width = w_in.shape
    ndt = width - OFF_DT
    assert 0 < ndt <= LANES and ndt % SUBLANES == 0 and OFF_DT % ndt == 0
    w_t = jnp.transpose(w_in[0])
    return pl.pallas_call(
        _cast_w_in_body,
        out_shape=(jax.ShapeDtypeStruct((rows, OFF_DT), BF16),
                   jax.ShapeDtypeStruct((rows, LANES), BF16)),
        grid=(OFF_DT // CAST_COLS,),
        in_specs=[pl.BlockSpec((CAST_COLS, rows), lambda i: (i, 0)),
                  pl.BlockSpec((ndt, rows), lambda i: (OFF_DT // ndt, 0))],
        out_specs=(pl.BlockSpec((rows, CAST_COLS), lambda i: (0, i)),
                   pl.BlockSpec((rows, LANES), lambda i: (0, 0))),
        compiler_params=_params(16),
        name="cast_w_in",
    )(w_t, w_t)


def _in_proj_body(x_ref, g_ref, w_ref, wdt_ref, o_ref):
    u = _rms(x_ref[...], g_ref[...]).astype(BF16)
    o_ref[:, 0:OFF_DT] = _mm(u, w_ref[...])
    o_ref[:, OFF_DT:PROJ_W] = _mm(u, wdt_ref[...])


def _in_proj(x, g, w, w_dt, row_tile):
    rows = x.shape[0]
    return pl.pallas_call(
        _in_proj_body,
        out_shape=jax.ShapeDtypeStruct((rows, PROJ_W), F32),
        grid=(rows // row_tile,),
        in_specs=[pl.BlockSpec((row_tile, D_MODEL), lambda i: (i, 0)),
                  _const_spec((1, D_MODEL)),
                  _const_spec((D_MODEL, OFF_DT)),
                  _const_spec((D_MODEL, LANES))],
        out_specs=pl.BlockSpec((row_tile, PROJ_W), lambda i: (i, 0)),
        compiler_params=_params(44),
        name="in_proj",
    )(x, g, w, w_dt)


def _out_up_body(cy_ref, h_ref, wo_ref, g_ref, wu_ref, h1_ref, up_ref):
    h1 = h_ref[...] + _mm(cy_ref[...], wo_ref[...])
    h1_ref[...] = h1
    up_ref[...] = _mm(_rms(h1, g_ref[...]).astype(BF16), wu_ref[...])


def _out_up(cy, h, w_out, g, w_up, row_tile):
    rows = h.shape[0]
    return pl.pallas_call(
        _out_up_body,
        out_shape=(jax.ShapeDtypeStruct((rows, D_MODEL), F32),
                   jax.ShapeDtypeStruct((rows, 2 * D_FF), F32)),
        grid=(rows // row_tile,),
        in_specs=[pl.BlockSpec((row_tile, 2 * D_CONV), lambda i: (i, 0)),
                  pl.BlockSpec((row_tile, D_MODEL), lambda i: (i, 0)),
                  _const_spec((2 * D_CONV, D_MODEL)),
                  _const_spec((1, D_MODEL)),
                  _const_spec((D_MODEL, 2 * D_FF))],
        out_specs=(pl.BlockSpec((row_tile, D_MODEL), lambda i: (i, 0)),
                   pl.BlockSpec((row_tile, 2 * D_FF), lambda i: (i, 0))),
        compiler_params=_params(52),
        name="out_up",
    )(cy, h, w_out, g, w_up)


def _split3(v):
    v1 = v.astype(BF16)
    r = v - v1.astype(F32)
    v2 = r.astype(BF16)
    v3 = (r - v2.astype(F32)).astype(BF16)
    return v1, v2, v3


def _ssd_intra(xs, bm, cm, dt, a, rows_per_seq, e2, d_exp, tick=lambda: None):
    tile = xs.shape[0]
    nseq = tile // rows_per_seq
    shift = rows_per_seq.bit_length() - 1
    row = lax.broadcasted_iota(jnp.int32, (tile, tile), 0)
    col = lax.broadcasted_iota(jnp.int32, (tile, tile), 1)
    causal = col <= row
    if nseq > 1:
        same = lax.shift_right_logical(row, shift) == lax.shift_right_logical(col, shift)
        mask = jnp.logical_and(same, causal)
        bd = jnp.where(same, 1.0, 0.0).astype(BF16)
    else:
        mask = causal
        bd = jnp.ones((tile, tile), BF16)
    tri = jnp.where(mask, 1.0, 0.0).astype(BF16)

    def mm(p, q, nt=False):
        out = _mm_nt(p, q) if nt else _mm(p, q)
        tick()
        return out

    a1, a2, a3 = _split3(a)
    acs = mm(tri, a1) + mm(tri, a2) + mm(tri, a3)
    tot = mm(bd, a1) + mm(bd, a2) + mm(bd, a3)
    acs_t = acs.T

    def expand(q):
        q1 = q.astype(BF16)
        q2 = (q - q1.astype(F32)).astype(BF16)
        return mm(jnp.concatenate([q1, q2], axis=1), e2)

    xdt = xs * expand(dt)
    xw_t = (xs * expand(dt * jnp.exp(tot - acs))).T.astype(BF16)
    seg_e = expand(jnp.exp(acs))

    lane = lax.broadcasted_iota(jnp.int32, (tile, LANES), 1)
    low_half = lane < HEAD_DIM
    ydiag = []
    for g in range(N_GROUPS):
        cgb = cm[:, g * D_STATE:(g + 1) * D_STATE].astype(BF16)
        bgb = bm[:, g * D_STATE:(g + 1) * D_STATE].astype(BF16)
        cb = mm(cgb, bgb, nt=True)
        for pair in range(HEADS_PER_GROUP // 2):
            blk = g * (HEADS_PER_GROUP // 2) + pair
            xb = xdt[:, blk * LANES:(blk + 1) * LANES]
            acc = None
            for half in range(2):
                h = 2 * blk + half
                diff = acs[:, h:h + 1] - acs_t[h:h + 1, :]
                decay = jnp.where(mask, jnp.exp(jnp.where(mask, diff, 0.0)), 0.0)
                xh = jnp.where(low_half if half == 0 else jnp.logical_not(low_half), xb, 0.0)
                term = mm((cb * decay).astype(BF16), xh.astype(BF16))
                acc = term if acc is None else acc + term
            ydiag.append(acc)
    return dict(base=jnp.concatenate(ydiag, axis=1) + xs * d_exp, xw_t=xw_t, seg_e=seg_e,
                tot=tot, bm=bm, cm=cm)


def _ssd_inter(part, z, rows_per_seq, snorm_g, h_get, h_set, tick=lambda: None):
    bm, cm, tot, xw_t = part["bm"], part["cm"], part["tot"], part["xw_t"]
    tile = bm.shape[0]
    nseq = tile // rows_per_seq
    shift = rows_per_seq.bit_length() - 1
    rowv = lax.broadcasted_iota(jnp.int32, (tile, D_STATE), 0)
    seq_of_row = lax.shift_right_logical(rowv, shift)
    yoff = []
    for g in range(N_GROUPS):
        cg = cm[:, g * D_STATE:(g + 1) * D_STATE]
        bg = bm[:, g * D_STATE:(g + 1) * D_STATE]
        acc = None
        for b in range(nseq):
            hprev = h_get(b, g)
            if nseq > 1:
                sel = seq_of_row == b
                c_b = jnp.where(sel, cg, 0.0).astype(BF16)
                b_b = jnp.where(sel, bg, 0.0).astype(BF16)
            else:
                c_b, b_b = cg.astype(BF16), bg.astype(BF16)
            term = _mm_nt(c_b, hprev.astype(BF16))
            tick()
            acc = term if acc is None else acc + term
            dec = jnp.exp(tot[b * rows_per_seq:b * rows_per_seq + 1, :])
            scaled = [hprev[e * HEAD_DIM:(e + 1) * HEAD_DIM, :]
                      * dec[:, g * HEADS_PER_GROUP + e:g * HEADS_PER_GROUP + e + 1]
                      for e in range(HEADS_PER_GROUP)]
            upd = _mm(xw_t[g * GROUP_W:(g + 1) * GROUP_W, :], b_b)
            tick()
            h_set(b, g, jnp.concatenate(scaled, axis=0) + upd)
        yoff.append(acc)

    y = part["base"] + jnp.concatenate(yoff, axis=1) * part["seg_e"]
    yz = y * (z * _sigmoid(z))
    outs = []
    for g in range(N_GROUPS):
        blk = yz[:, g * GROUP_W:(g + 1) * GROUP_W]
        ms = jnp.mean(blk * blk, axis=-1, keepdims=True)
        outs.append(blk * lax.rsqrt(ms + EPS) * snorm_g[:, g * GROUP_W:(g + 1) * GROUP_W])
    return jnp.concatenate(outs, axis=1)


def _dt_and_a(dt_raw, dtb, alog, pad_rows, valid_rows_per_tile=None):
    v = dt_raw + dtb
    dt = jnp.maximum(v, 0.0) + jnp.log1p(jnp.exp(-jnp.abs(v)))
    lane = lax.broadcasted_iota(jnp.int32, dt.shape, 1)
    keep = lane < N_HEADS
    row = lax.broadcasted_iota(jnp.int32, dt.shape, 0)
    if pad_rows:
        keep = jnp.logical_and(row >= pad_rows, keep)
    if valid_rows_per_tile is not None:
        keep = jnp.logical_and((row & (SAMPLE_ROWS - 1)) < valid_rows_per_tile, keep)
    dt = jnp.where(keep, dt, 0.0)
    return dt, dt * (-jnp.exp(alog))


def _layernorm_silu(cv, g, b):
    mu = jnp.mean(cv, axis=-1, keepdims=True)
    d = cv - mu
    var = jnp.mean(d * d, axis=-1, keepdims=True)
    cn = d * lax.rsqrt(var + EPS) * g + b
    return cn * _sigmoid(cn)


CONV_ROWS = 32
XCONV_ROWS = 64
CONF_PAD = 32
XBC_PAD = 8
PROJ_COLS = 256


def _lane_blocks(width):
    return width // LANES


def _mixer_p_body(pad_rows, nchunk, *refs):
    s = pl.program_id(0)
    nshared = 4
    sets = refs[-nshared - 8:-nshared]

    @pl.when(s == 0)
    def _():
        for ref in sets:
            ref[...] = jnp.zeros(ref.shape, F32)

    for slot_p in range(2):
        @pl.when(lax.rem(s, 2) == slot_p)
        def _(slot_p=slot_p):
            _mixer_p_step(slot_p, pad_rows, nchunk, *refs)


def _mixer_p_step(slot_p, pad_rows, nchunk, flag_ref, xp_ref, xm_ref, gm_ref, win_ref, wdt_ref, conf0_ref, xbc0_ref,
                  ssm0_ref, cw_ref, cb_ref, lng_ref, lnb_ref, sw_ref, sb_ref, dtb_ref, alog_ref,
                  dexp_ref, sng_ref, e2_ref, wout_ref, h1_ref, conf_out_ref, xbc_out_ref,
                  ssm_out_ref, gbuf0, xbuf0, zbuf0, dtbuf0, gbuf1, xbuf1, zbuf1, dtbuf1,
                  hst, cbuf, xact, cy):
    s = pl.program_id(0)
    m = jnp.maximum(s - 1, 0)
    c = lax.rem(m, nchunk)
    last = nchunk - 1
    tile = xp_ref.shape[0]
    conf_lo = CONF_PAD - (CONF_K - 1)
    xbc_lo = XBC_PAD - (SSM_K - 1)
    sets = ((gbuf0, xbuf0, zbuf0, dtbuf0), (gbuf1, xbuf1, zbuf1, dtbuf1))
    gnext, xnext, znext, dtnext = sets[slot_p]
    gbuf, xbuf, zbuf, dtbuf = sets[1 - slot_p]

    @pl.when(c == 0)
    def _():
        for j in range(_lane_blocks(D_CONV)):
            gbuf[j, conf_lo:CONF_PAD, :] = conf0_ref[0, :, j * LANES:(j + 1) * LANES]
        for j in range(_lane_blocks(XBC)):
            xbuf[j, xbc_lo:XBC_PAD, :] = xbc0_ref[0, :, j * LANES:(j + 1) * LANES]

    u = _rms(xp_ref[...], gm_ref[...]).astype(BF16)
    per = PROJ_COLS // LANES

    def proj_glu(i):
        lin = _mm(u, win_ref[:, i * PROJ_COLS:(i + 1) * PROJ_COLS])
        gate = _mm(u, win_ref[:, OFF_GATE + i * PROJ_COLS:OFF_GATE + (i + 1) * PROJ_COLS])
        glu = lin * _sigmoid(gate)
        for jj in range(per):
            gnext[i * per + jj, CONF_PAD:CONF_PAD + tile, :] = glu[:, jj * LANES:(jj + 1) * LANES]

    def proj_z(i):
        znext[:, i * PROJ_COLS:(i + 1) * PROJ_COLS] = _mm(
            u, win_ref[:, OFF_Z + i * PROJ_COLS:OFF_Z + (i + 1) * PROJ_COLS])

    def proj_xbc(i):
        res = _mm(u, win_ref[:, OFF_XBC + i * PROJ_COLS:OFF_XBC + (i + 1) * PROJ_COLS])
        for jj in range(per):
            xnext[i * per + jj, XBC_PAD:XBC_PAD + tile, :] = res[:, jj * LANES:(jj + 1) * LANES]

    def proj_dt():
        dtnext[...] = _mm(u, wdt_ref[...])

    stage1 = ([(2, functools.partial(proj_glu, i)) for i in range(D_CONV // PROJ_COLS)]
              + [(1, functools.partial(proj_z, i)) for i in range(D_SSM // PROJ_COLS)]
              + [(1, functools.partial(proj_xbc, i)) for i in range(XBC // PROJ_COLS)]
              + [(1, proj_dt)])

    x = xm_ref[...]
    dt_raw = dtbuf[...]

    chain = []

    def conv_start(bias, rows):
        init = jnp.broadcast_to(bias, (rows, LANES))
        if chain:
            never = jnp.broadcast_to(flag_ref[0], (rows, LANES)) != 0
            init = jnp.where(never, chain[-1][0:rows, :], init)
        return init

    def conv_conf(rb, j):
        cols = slice(j * LANES, (j + 1) * LANES)
        acc = conv_start(cb_ref[:, cols], CONV_ROWS)
        for k in range(CONF_K):
            r0 = rb * CONV_ROWS + conf_lo + k
            acc = acc + cw_ref[k:k + 1, cols] * gbuf[j, r0:r0 + CONV_ROWS, :]
        cbuf[rb * CONV_ROWS:(rb + 1) * CONV_ROWS, cols] = acc
        chain.append(acc)

    def conv_xbc(rb, j):
        cols = slice(j * LANES, (j + 1) * LANES)
        acc = conv_start(sb_ref[:, cols], XCONV_ROWS)
        for k in range(SSM_K):
            r0 = rb * XCONV_ROWS + xbc_lo + k
            acc = acc + sw_ref[k:k + 1, cols] * xbuf[j, r0:r0 + XCONV_ROWS, :]
        xact[rb * XCONV_ROWS:(rb + 1) * XCONV_ROWS, cols] = acc * _sigmoid(acc)
        chain.append(acc)

    for _, item in stage1:
        item()
    assert XCONV_ROWS >= CONV_ROWS
    for rb in range(tile // XCONV_ROWS):
        for j in range(_lane_blocks(XBC)):
            conv_xbc(rb, j)
    for rb in range(tile // CONV_ROWS):
        for j in range(_lane_blocks(D_CONV)):
            conv_conf(rb, j)

    @pl.when(c == 0)
    def _():
        hst[...] = ssm0_ref[0].reshape(N_HEADS * HEAD_DIM, D_STATE)

    cy[:, 0:D_CONV] = _layernorm_silu(cbuf[...], lng_ref[...], lnb_ref[...]).astype(BF16)
    gnext[:, 0:CONF_PAD, :] = gbuf[:, tile:tile + CONF_PAD, :]
    xnext[:, 0:XBC_PAD, :] = xbuf[:, tile:tile + XBC_PAD, :]

    dt, a = _dt_and_a(dt_raw, dtb_ref[...], alog_ref[...], pad_rows)

    def h_get(b, g):
        return hst[g * GROUP_W:(g + 1) * GROUP_W, :]

    def h_set(b, g, val):
        hst[g * GROUP_W:(g + 1) * GROUP_W, :] = val

    for j in range(tile // SSD_CHUNK):
        rs = slice(j * SSD_CHUNK, (j + 1) * SSD_CHUNK)
        part = _ssd_intra(xact[rs, 0:D_SSM], xact[rs, D_SSM:D_SSM + GN], xact[rs, D_SSM + GN:XBC],
                          dt[rs], a[rs], SSD_CHUNK, e2_ref[...], dexp_ref[...])
        out = _ssd_inter(part, zbuf[rs, :], SSD_CHUNK, sng_ref[...], h_get, h_set)
        cy[rs, D_CONV:2 * D_CONV] = out.astype(BF16)
        h1_ref[rs, :] = x[rs, :] + _mm(cy[rs, :], wout_ref[...])

    @pl.when(c == last)
    def _():
        for j in range(_lane_blocks(D_CONV)):
            conf_out_ref[0, :, j * LANES:(j + 1) * LANES] = gbuf[j, tile + conf_lo:tile + CONF_PAD, :]
        for j in range(_lane_blocks(XBC)):
            xbc_out_ref[0, :, j * LANES:(j + 1) * LANES] = xbuf[j, tile + xbc_lo:tile + XBC_PAD, :]
        ssm_out_ref[0] = hst[...].reshape(N_HEADS, HEAD_DIM, D_STATE)


def _mixer_p(x, conf0, xbc0, ssm0, p, nbatch, tile, pad_rows):
    rows = x.shape[0]
    ntile = rows // tile
    nchunk = ntile // nbatch
    stage1_map = lambda s: (jnp.minimum(s, ntile - 1), 0)
    stage2_map = lambda s: (jnp.maximum(s - 1, 0), 0)
    state_specs = lambda shp: pl.BlockSpec(
        (1,) + shp, lambda s: (jnp.maximum(s - 1, 0) // nchunk,) + (0,) * len(shp))
    small = [p["conf_w"], p["conf_b"], p["ln_g"], p["ln_b"], p["sconv_w"], p["sconv_b"],
             p["dt_bias"], p["a_log"], p["d_exp"], p["snorm_g"], p["e2"]]
    return pl.pallas_call(
        functools.partial(_mixer_p_body, pad_rows, nchunk),
        out_shape=(jax.ShapeDtypeStruct((rows, D_MODEL), F32),
                   jax.ShapeDtypeStruct((nbatch, CONF_K - 1, D_CONV), F32),
                   jax.ShapeDtypeStruct((nbatch, SSM_K - 1, XBC), F32),
                   jax.ShapeDtypeStruct((nbatch, N_HEADS, HEAD_DIM, D_STATE), F32)),
        grid=(ntile + 1,),
        in_specs=[pl.BlockSpec(memory_space=pltpu.SMEM),
                  pl.BlockSpec((tile, D_MODEL), stage1_map),
                  pl.BlockSpec((tile, D_MODEL), stage2_map),
                  _const_spec((1, D_MODEL)),
                  _const_spec((D_MODEL, OFF_DT)),
                  _const_spec((D_MODEL, LANES)),
                  _const_spec((1, CONF_K - 1, D_CONV)),
                  _const_spec((1, SSM_K - 1, XBC)),
                  _const_spec((1, N_HEADS, HEAD_DIM, D_STATE))]
                 + [_const_spec(s.shape) for s in small]
                 + [_const_spec((2 * D_CONV, D_MODEL))],
        out_specs=(pl.BlockSpec((tile, D_MODEL), stage2_map),
                   state_specs((CONF_K - 1, D_CONV)),
                   state_specs((SSM_K - 1, XBC)),
                   state_specs((N_HEADS, HEAD_DIM, D_STATE))),
        scratch_shapes=2 * [pltpu.VMEM((_lane_blocks(D_CONV), tile + CONF_PAD, LANES), F32),
                            pltpu.VMEM((_lane_blocks(XBC), tile + XBC_PAD, LANES), F32),
                            pltpu.VMEM((tile, D_SSM), F32),
                            pltpu.VMEM((tile, LANES), F32)]
                       + [pltpu.VMEM((N_HEADS * HEAD_DIM, D_STATE), F32),
                        pltpu.VMEM((tile, D_CONV), F32),
                        pltpu.VMEM((tile, XBC), F32),
                        pltpu.VMEM((tile, 2 * D_CONV), BF16)],
        compiler_params=_params(48),
        name="mixer_prompt",
    )(jnp.zeros((1,), jnp.int32), x, x, p["g_mix"], p["w_in"], p["w_dt"], conf0, xbc0, ssm0,
      *small, p["w_out"])


def _mix_s_body(proj_ref, conf_ref, xbc_ref, ssm_ref, cw_ref, cb_ref, lng_ref, lnb_ref, sw_ref,
                sb_ref, dtb_ref, alog_ref, dexp_ref, sng_ref, e2_ref,
                cy_ref, conf_out_ref, xbc_out_ref, ssm_out_ref, stage, tbuf, cbuf, xact):
    nseq = ssm_ref.shape[0]
    rows = nseq * SAMPLE_ROWS
    new_rows = SAMPLE_ROWS // 2

    def rows_at(t):
        return pl.ds(t, nseq, stride=SAMPLE_ROWS)

    def conv_block(value, state_ref, state_out_ref, w_ref, b_ref, cols, out_ref, silu):
        nstate = state_ref.shape[0]
        stage[...] = value
        for i in range(nstate):
            tbuf[i] = state_ref[i, :, cols]
        for t in range(new_rows):
            tbuf[nstate + t] = stage[rows_at(t), :]
        stage[...] = jnp.zeros((rows, LANES), F32)
        for t in range(new_rows):
            acc = jnp.broadcast_to(b_ref[:, cols], (nseq, LANES))
            for k in range(nstate + 1):
                acc = acc + w_ref[k:k + 1, cols] * tbuf[t + k]
            stage[rows_at(t), :] = acc * _sigmoid(acc) if silu else acc
        out_ref[:, cols] = stage[...]
        for i in range(nstate):
            state_out_ref[i, :, cols] = tbuf[i + new_rows]

    for j in range(_lane_blocks(D_CONV)):
        cols = slice(j * LANES, (j + 1) * LANES)
        gate = proj_ref[:, OFF_GATE + j * LANES:OFF_GATE + (j + 1) * LANES]
        conv_block(proj_ref[:, cols] * _sigmoid(gate), conf_ref, conf_out_ref, cw_ref, cb_ref,
                   cols, cbuf, False)
    cy_ref[:, 0:D_CONV] = _layernorm_silu(cbuf[...], lng_ref[...], lnb_ref[...]).astype(BF16)

    for j in range(_lane_blocks(XBC)):
        cols = slice(j * LANES, (j + 1) * LANES)
        conv_block(proj_ref[:, OFF_XBC + j * LANES:OFF_XBC + (j + 1) * LANES], xbc_ref,
                   xbc_out_ref, sw_ref, sb_ref, cols, xact, True)
    xv = xact[...]

    dt, a = _dt_and_a(proj_ref[:, OFF_DT:PROJ_W], dtb_ref[...], alog_ref[...], 0, new_rows)

    def h_get(b, g):
        return ssm_ref[b, g * HEADS_PER_GROUP:(g + 1) * HEADS_PER_GROUP].reshape(GROUP_W, D_STATE)

    def h_set(b, g, val):
        ssm_out_ref[b, g * HEADS_PER_GROUP:(g + 1) * HEADS_PER_GROUP] = val.reshape(
            HEADS_PER_GROUP, HEAD_DIM, D_STATE)

    part = _ssd_intra(xv[:, 0:D_SSM], xv[:, D_SSM:D_SSM + GN], xv[:, D_SSM + GN:XBC], dt, a,
                      SAMPLE_ROWS, e2_ref[...], dexp_ref[...])
    out = _ssd_inter(part, proj_ref[:, OFF_Z:OFF_XBC], SAMPLE_ROWS, sng_ref[...], h_get, h_set)
    cy_ref[:, D_CONV:2 * D_CONV] = out.astype(BF16)


def _mix_s(proj, conf, xbc, ssm, p):
    nbatch = ssm.shape[0]
    rows = SAMPLE_SEQS * SAMPLE_ROWS
    small = [p["conf_w"], p["conf_b"], p["ln_g"], p["ln_b"], p["sconv_w"], p["sconv_b"],
             p["dt_bias"], p["a_log"], p["d_exp"], p["snorm_g"], p["e2"]]
    blk3 = lambda shp: pl.BlockSpec((SAMPLE_SEQS,) + shp, lambda i: (i,) + (0,) * len(shp))
    tmaj = lambda nrow, width: pl.BlockSpec((nrow, SAMPLE_SEQS, width), lambda i: (0, i, 0))
    return pl.pallas_call(
        _mix_s_body,
        out_shape=(jax.ShapeDtypeStruct((nbatch * SAMPLE_ROWS, 2 * D_CONV), BF16),
                   jax.ShapeDtypeStruct(conf.shape, F32),
                   jax.ShapeDtypeStruct(xbc.shape, F32),
                   jax.ShapeDtypeStruct(ssm.shape, F32)),
        grid=(nbatch // SAMPLE_SEQS,),
        in_specs=[pl.BlockSpec((rows, PROJ_W), lambda i: (i, 0)),
                  tmaj(CONF_K - 1, D_CONV), tmaj(SSM_K - 1, XBC),
                  blk3((N_HEADS, HEAD_DIM, D_STATE))]
                 + [_const_spec(s.shape) for s in small],
        out_specs=(pl.BlockSpec((rows, 2 * D_CONV), lambda i: (i, 0)),
                   tmaj(CONF_K - 1, D_CONV), tmaj(SSM_K - 1, XBC),
                   blk3((N_HEADS, HEAD_DIM, D_STATE))),
        scratch_shapes=[pltpu.VMEM((rows, LANES), F32),
                        pltpu.VMEM((CONF_K - 1 + SAMPLE_ROWS // 2, SAMPLE_SEQS, LANES), F32),
                        pltpu.VMEM((rows, D_CONV), F32),
                        pltpu.VMEM((rows, XBC), F32)],
        compiler_params=_params(40),
        name="mix_sample",
    )(proj, conf, xbc, ssm, *small)


FFN_COLS = 256
FFN_PAD = 8


def _ffn_p_body(nchunk, *refs):
    s = pl.program_id(0)
    bufs = refs[-3:-1]

    @pl.when(s == 0)
    def _():
        for ref in bufs:
            ref[...] = jnp.zeros(ref.shape, F32)

    for slot_p in range(2):
        @pl.when(lax.rem(s, 2) == slot_p)
        def _(slot_p=slot_p):
            _ffn_p_step(slot_p, nchunk, *refs)


def _ffn_p_step(slot_p, nchunk, flag_ref, hp_ref, hm_ref, gn_ref, wu_ref, tail0_ref, cw_ref, cb_ref,
                wd_ref, gf_ref, y_ref, tail_out_ref, ubuf0, ubuf1, act):
    s = pl.program_id(0)
    c = lax.rem(jnp.maximum(s - 1, 0), nchunk)
    tile = hp_ref.shape[0]
    lo = FFN_PAD - (FFN_K - 1)
    nblk = _lane_blocks(2 * D_FF)
    half = _lane_blocks(D_FF)
    unext = (ubuf0, ubuf1)[slot_p]
    ubuf = (ubuf0, ubuf1)[1 - slot_p]

    @pl.when(c == 0)
    def _():
        for j in range(nblk):
            ubuf[j, lo:FFN_PAD, :] = tail0_ref[0, :, j * LANES:(j + 1) * LANES]

    u = _rms(hp_ref[...], gn_ref[...]).astype(BF16)
    per = PROJ_COLS // LANES
    for i in range(2 * D_FF // PROJ_COLS):
        res = _mm(u, wu_ref[:, i * PROJ_COLS:(i + 1) * PROJ_COLS])
        for jj in range(per):
            unext[i * per + jj, FFN_PAD:FFN_PAD + tile, :] = res[:, jj * LANES:(jj + 1) * LANES]

    def conv(j, r0, init):
        cols = slice(j * LANES, (j + 1) * LANES)
        out = init
        for k in range(FFN_K):
            out = out + cw_ref[k:k + 1, cols] * ubuf[j, r0 + lo + k:r0 + lo + k + XCONV_ROWS, :]
        return out

    def bias(j):
        return jnp.broadcast_to(cb_ref[:, j * LANES:(j + 1) * LANES], (XCONV_ROWS, LANES))

    prev = None
    for rb in range(tile // XCONV_ROWS):
        r0 = rb * XCONV_ROWS
        for j in range(half):
            init = bias(j)
            if prev is not None:
                never = jnp.broadcast_to(flag_ref[0], (XCONV_ROWS, LANES)) != 0
                init = jnp.where(never, prev, init)
            gate = conv(j, r0, init)
            prev = gate * _sigmoid(gate) * conv(half + j, r0, bias(half + j))
            act[r0:r0 + XCONV_ROWS, j * LANES:(j + 1) * LANES] = prev.astype(BF16)

    @pl.when(c == nchunk - 1)
    def _():
        for j in range(nblk):
            tail_out_ref[0, :, j * LANES:(j + 1) * LANES] = ubuf[j, tile + lo:tile + FFN_PAD, :]

    unext[:, 0:FFN_PAD, :] = ubuf[:, tile:tile + FFN_PAD, :]
    y_ref[...] = _rms(hm_ref[...] + _mm(act[...], wd_ref[...]), gf_ref[...])


def _ffn_p(h1, tail0, p, nbatch, tile):
    rows = h1.shape[0]
    ntile = rows // tile
    nchunk = ntile // nbatch
    stage1_map = lambda s: (jnp.minimum(s, ntile - 1), 0)
    stage2_map = lambda s: (jnp.maximum(s - 1, 0), 0)
    return pl.pallas_call(
        functools.partial(_ffn_p_body, nchunk),
        out_shape=(jax.ShapeDtypeStruct((rows, D_MODEL), F32),
                   jax.ShapeDtypeStruct((nbatch, FFN_K - 1, 2 * D_FF), F32)),
        grid=(ntile + 1,),
        in_specs=[pl.BlockSpec(memory_space=pltpu.SMEM),
                  pl.BlockSpec((tile, D_MODEL), stage1_map),
                  pl.BlockSpec((tile, D_MODEL), stage2_map),
                  _const_spec((1, D_MODEL)),
                  _const_spec((D_MODEL, 2 * D_FF)),
                  _const_spec((1, FFN_K - 1, 2 * D_FF)),
                  _const_spec((FFN_K, 2 * D_FF)),
                  _const_spec((1, 2 * D_FF)),
                  _const_spec((D_FF, D_MODEL)),
                  _const_spec((1, D_MODEL))],
        out_specs=(pl.BlockSpec((tile, D_MODEL), stage2_map),
                   pl.BlockSpec((1, FFN_K - 1, 2 * D_FF),
                                lambda s: (jnp.maximum(s - 1, 0) // nchunk, 0, 0))),
        scratch_shapes=2 * [pltpu.VMEM((_lane_blocks(2 * D_FF), tile + FFN_PAD, LANES), F32)]
                       + [pltpu.VMEM((tile, D_FF), BF16)],
        compiler_params=_params(48),
        name="ffn_prompt",
    )(jnp.zeros((1,), jnp.int32), h1, h1, p["g_ffn"], p["w_up"], tail0, p["fconv_w"],
      p["fconv_b"], p["w_down"], p["g_final"])


def _ffn_s_body(up_ref, h1_ref, st_ref, cw_ref, cb_ref, wd_ref, gf_ref,
                y_ref, st_out_ref, ubuf, act):
    nseq = st_ref.shape[0]
    rows = nseq * SAMPLE_ROWS
    new_rows = SAMPLE_ROWS // 2
    lo = FFN_PAD - (FFN_K - 1)
    ubuf[:, lo:FFN_PAD, :] = st_ref[...]
    ubuf[:, FFN_PAD:FFN_PAD + SAMPLE_ROWS, :] = up_ref[...].reshape(nseq, SAMPLE_ROWS, 2 * D_FF)

    width = LANES

    def conv(cols):
        out = jnp.broadcast_to(cb_ref[:, cols].reshape(1, 1, width), (nseq, SAMPLE_ROWS, width))
        for k in range(FFN_K):
            w = cw_ref[k:k + 1, cols].reshape(1, 1, width)
            out = out + w * ubuf[:, lo + k:lo + k + SAMPLE_ROWS, cols]
        return out.reshape(rows, width)

    for j in range(D_FF // width):
        gate = conv(slice(j * width, (j + 1) * width))
        lin = conv(slice(D_FF + j * width, D_FF + (j + 1) * width))
        act[:, j * width:(j + 1) * width] = (gate * _sigmoid(gate) * lin).astype(BF16)

    st_out_ref[...] = ubuf[:, lo + new_rows:FFN_PAD + new_rows, :]
    y_ref[...] = _rms(h1_ref[...] + _mm(act[...], wd_ref[...]), gf_ref[...])


def _ffn_s(up, h1, st, p):
    nbatch = st.shape[0]
    nseq = FFN_SAMPLE_SEQS
    assert nbatch % nseq == 0
    rows = nseq * SAMPLE_ROWS
    return pl.pallas_call(
        _ffn_s_body,
        out_shape=(jax.ShapeDtypeStruct((nbatch * SAMPLE_ROWS, D_MODEL), F32),
                   jax.ShapeDtypeStruct(st.shape, F32)),
        grid=(nbatch // nseq,),
        in_specs=[pl.BlockSpec((rows, 2 * D_FF), lambda i: (i, 0)),
                  pl.BlockSpec((rows, D_MODEL), lambda i: (i, 0)),
                  pl.BlockSpec((nseq, FFN_K - 1, 2 * D_FF), lambda i: (i, 0, 0)),
                  _const_spec((FFN_K, 2 * D_FF)),
                  _const_spec((1, 2 * D_FF)),
                  _const_spec((D_FF, D_MODEL)),
                  _const_spec((1, D_MODEL))],
        out_specs=(pl.BlockSpec((rows, D_MODEL), lambda i: (i, 0)),
                   pl.BlockSpec((nseq, FFN_K - 1, 2 * D_FF), lambda i: (i, 0, 0))),
        scratch_shapes=[pltpu.VMEM((nseq, FFN_PAD + SAMPLE_ROWS, 2 * D_FF), F32),
                        pltpu.VMEM((rows, D_FF), BF16)],
        compiler_params=_params(40),
        name="ffn_sample",
    )(up, h1, st, p["fconv_w"], p["fconv_b"], p["w_down"], p["g_final"])


def _head_expander():
    k = lax.broadcasted_iota(jnp.int32, (2 * LANES, D_SSM), 0) % LANES
    j = lax.broadcasted_iota(jnp.int32, (2 * LANES, D_SSM), 1) // HEAD_DIM
    return (k == j).astype(BF16)


def _lane_pad(v):
    return jnp.pad(v.astype(F32), (0, LANES - v.shape[0])).reshape(1, LANES)


def kernel(x_prompt, x_sample, state_conf_conv, state_xbc_conv, state_ssm, state_ffn_conv, meta_tokens, norm_mix_g, w_in, conf_conv_w, conf_conv_b, conf_ln_g, conf_ln_b, ssm_conv_w, ssm_conv_b, dt_bias, a_log, d_skip, ssm_norm_g, w_out, norm_ffn_g, w_up, ffn_conv_w, ffn_conv_b, w_down, norm_final_g):
    depth = w_in.shape[0]
    assert depth == 1, "one layer per step"
    nb, seq, _ = x_prompt.shape
    nsb, sseq, _ = x_sample.shape
    assert 2 * sseq == SAMPLE_ROWS and nsb % SAMPLE_SEQS == 0 and N_META <= SSD_CHUNK
    row = lambda v: v.astype(F32).reshape(1, -1)
    p = dict(
        conf_w=conf_conv_w[0], conf_b=row(conf_conv_b[0]), ln_g=row(conf_ln_g[0]),
        ln_b=row(conf_ln_b[0]), sconv_w=ssm_conv_w[0], sconv_b=row(ssm_conv_b[0]),
        dt_bias=_lane_pad(dt_bias[0]), a_log=_lane_pad(a_log[0]),
        d_exp=row(jnp.repeat(d_skip[0], HEAD_DIM)), snorm_g=row(ssm_norm_g[0]),
        e2=_head_expander(), fconv_w=ffn_conv_w[0], fconv_b=row(ffn_conv_b[0]),
        w_down=w_down[0].astype(BF16), g_final=row(norm_final_g),
        g_mix=row(norm_mix_g[0]), g_ffn=row(norm_ffn_g[0]),
        w_out=w_out[0].astype(BF16), w_up=w_up[0].astype(BF16))
    p["w_in"], p["w_dt"] = _cast_w_in(w_in)
    g_mix, g_ffn, w_out_b, w_up_b = p["g_mix"], p["g_ffn"], p["w_out"], p["w_up"]

    meta = jnp.concatenate([jnp.zeros((SSD_CHUNK - N_META, D_MODEL), F32),
                            meta_tokens.astype(F32)], axis=0)
    h1_m, conf0, xbc0, ssm0 = _mixer_p(
        meta, jnp.zeros((1, CONF_K - 1, D_CONV), F32), jnp.zeros((1, SSM_K - 1, XBC), F32),
        jnp.zeros((1, N_HEADS, HEAD_DIM, D_STATE), F32), p, 1, SSD_CHUNK, SSD_CHUNK - N_META)
    _, ffn0 = _ffn_p(h1_m, jnp.zeros((1, FFN_K - 1, 2 * D_FF), F32), p, 1, SSD_CHUNK)

    tile = 256
    xp = x_prompt.reshape(nb * seq, D_MODEL)
    h1_p, p_conf, p_xbc, p_ssm = _mixer_p(xp, conf0, xbc0, ssm0, p, nb, tile, 0)
    y_p, p_ffn = _ffn_p(h1_p, ffn0, p, nb, tile)

    xs = jnp.pad(x_sample, ((0, 0), (0, SAMPLE_ROWS - sseq), (0, 0))).reshape(
        nsb * SAMPLE_ROWS, D_MODEL)
    proj_s = _in_proj(xs, g_mix, p["w_in"], p["w_dt"], tile)
    time_major = lambda v: jnp.transpose(v, (1, 0, 2))
    cy_s, s_conf, s_xbc, s_ssm = _mix_s(proj_s, time_major(state_conf_conv[0]),
                                        time_major(state_xbc_conv[0]), state_ssm[0], p)
    s_conf, s_xbc = time_major(s_conf), time_major(s_xbc)
    h1_s, up_s = _out_up(cy_s, xs, w_out_b, g_ffn, w_up_b, tile)
    y_s, s_ffn = _ffn_s(up_s, h1_s, state_ffn_conv[0], p)

    y_prompt = y_p.reshape(nb, seq, D_MODEL)
    y_sample = y_s.reshape(nsb, SAMPLE_ROWS, D_MODEL)[:, :sseq]
    return (y_prompt, y_sample, p_conf[None], p_xbc[None], p_ssm[None], p_ffn[None],
            s_conf[None], s_xbc[None], s_ssm[None], s_ffn[None])
```
